```python
import math
import numpy as np
import jax
import jax.numpy as jnp
from jax import lax

D_MODEL = 1024
BATCH = 8
SEQ = 4096
DEPTH = 2

CTX_LEN = 256
GRID_W = 64
N_EVEN = (DEPTH + 1) // 2
N_ODD = DEPTH // 2
N_MOD = 9
D_FF = 2816
EPS = 1e-6

DA_HEADS = 4
DA_HD = 64
DA_VD = 2 * DA_HD
ML_HEADS = 4
ML_QK = 64
ML_V = 128
ML_CONV = 3
GLA_HEADS = 4
GLA_K = 128
GLA_V = 256
GLA_RANK = 16
GLA_TAU = 16.0

CHUNK = 64
Q_BLOCK = 128
ROPE_BASE = 10000.0
ROPE_AXIS = DA_HD // 2

EVEN_SPLITS = (DA_HEADS * 2 * DA_HD, DA_HEADS * 2 * DA_HD, DA_HEADS * DA_VD,
               2 * ML_HEADS * ML_QK, ML_HEADS * ML_V, ML_HEADS * ML_V, 4 * ML_HEADS)
EVEN_IN = sum(EVEN_SPLITS)
ODD_SPLITS = (GLA_HEADS * GLA_K, GLA_HEADS * GLA_K, GLA_HEADS * GLA_V, GLA_HEADS * GLA_V, 2 * GLA_RANK)
ODD_IN = sum(ODD_SPLITS)

kernel_name = "hybrid_diffattn_mlstm_gla_macaron_prefix"


def _split(p, sizes):
    idx = np.cumsum(sizes)[:-1].tolist()
    return jnp.split(p, idx, axis=-1)


def rms_norm(x, g):
    xf = x.astype(jnp.float32)
    y = xf * lax.rsqrt(jnp.mean(xf * xf, axis=-1, keepdims=True) + EPS)
    return y.astype(x.dtype) * g


def ada_norm(x, g, shift, scale):
    return rms_norm(x, g) * (1.0 + scale) + shift


def swiglu(h, w_in, w_out):
    a, b = jnp.split(h @ w_in, 2, axis=-1)
    return (jax.nn.silu(a) * b) @ w_out


def axial_rope_tables(rows, dtype):
    row = jnp.repeat(jnp.arange(rows), GRID_W).astype(jnp.float32)
    col = (jnp.arange(rows * GRID_W) % GRID_W).astype(jnp.float32)
    inv = ROPE_BASE ** (-jnp.arange(ROPE_AXIS // 2, dtype=jnp.float32) * 2.0 / ROPE_AXIS)
    ar = row[:, None] * inv
    ac = col[:, None] * inv
    return (jnp.cos(ar).astype(dtype), jnp.sin(ar).astype(dtype),
            jnp.cos(ac).astype(dtype), jnp.sin(ac).astype(dtype))


def _rope_1d(x, cos, sin):
    cos = cos[None, :, None, None, :]
    sin = sin[None, :, None, None, :]
    x1, x2 = jnp.split(x, 2, axis=-1)
    return jnp.concatenate([x1 * cos - x2 * sin, x2 * cos + x1 * sin], axis=-1)


def rope_2d(x, rope):
    cr, sr, cc, sc = rope
    xr, xc = jnp.split(x, 2, axis=-1)
    return jnp.concatenate([_rope_1d(xr, cr, sr), _rope_1d(xc, cc, sc)], axis=-1)


def diff_softmax_core(q, k, v, lam):
    s = jnp.einsum('bqhcd,bkhcd->bhcqk', q.astype(jnp.float32), k.astype(jnp.float32)) * (DA_HD ** -0.5)
    p = jax.nn.softmax(s, axis=-1)
    w = p[:, :, 0] - lam * p[:, :, 1]
    return jnp.einsum('bhqk,bkhe->bqhe', w, v.astype(jnp.float32))


def diff_attn_latent(q, k, v, lam):
    b, t, h, _, d = q.shape
    nb = t // Q_BLOCK
    qb = jnp.moveaxis(q.reshape(b, nb, Q_BLOCK, h, 2, d), 1, 0)
    out = lax.map(lambda qq: diff_softmax_core(qq, k, v, lam), qb)
    return jnp.moveaxis(out, 0, 1).reshape(b, t, h, -1)


def dw_conv(x, w, bias):
    ch = x.shape[-1]
    pad = ML_CONV // 2
    y = lax.conv_general_dilated(x, w[:, None, :].astype(x.dtype), window_strides=(1,),
                                 padding=((pad, pad),), dimension_numbers=('NWC', 'WIO', 'NWC'),
                                 feature_group_count=ch)
    return y + bias


def to_chunks(a):
    b, s, h = a.shape[:3]
    rest = a.shape[3:]
    a = a.reshape((b, s // CHUNK, CHUNK, h) + rest)
    return a.transpose((1, 0, 3, 2) + tuple(range(4, a.ndim)))


def from_chunks(y):
    nc, b, h, l = y.shape[:4]
    rest = y.shape[4:]
    y = y.transpose((1, 0, 3, 2) + tuple(range(4, y.ndim)))
    return y.reshape((b, nc * l, h) + rest)


def mlstm_scan(q, k, v, ig, lf, state):
    xs = tuple(to_chunks(a.astype(jnp.float32)) for a in (q, k, v, ig, lf))
    tril = jnp.tril(jnp.ones((CHUNK, CHUNK), dtype=bool))

    def step(carry, inp):
        cmat, nvec, m = carry
        qc, kc, vc, ic, fc = inp
        bcum = jnp.cumsum(fc, axis=-1)
        dmat = jnp.where(tril, bcum[..., :, None] - bcum[..., None, :] + ic[..., None, :], -jnp.inf)
        m_inter = bcum + m[..., None]
        m_t = jnp.maximum(m_inter, jnp.max(dmat, axis=-1))
        w_inter = jnp.exp(m_inter - m_t)
        s = jnp.einsum('bhtd,bhsd->bhts', qc, kc) * jnp.exp(dmat - m_t[..., None])
        num = jnp.einsum('bhts,bhse->bhte', s, vc) + w_inter[..., None] * jnp.einsum('bhtd,bhed->bhte', qc, cmat)
        den = jnp.sum(s, axis=-1) + w_inter * jnp.einsum('bhtd,bhd->bht', qc, nvec)
        h = num / jnp.maximum(jnp.abs(den), jnp.exp(-m_t))[..., None]
        bl = bcum[..., -1]
        g = bl[..., None] - bcum + ic
        m_new = jnp.maximum(bl + m, jnp.max(g, axis=-1))
        decay = jnp.exp(bl + m - m_new)
        wk = jnp.exp(g - m_new[..., None])
        cmat = decay[..., None, None] * cmat + jnp.einsum('bhs,bhse,bhsd->bhed', wk, vc, kc)
        nvec = decay[..., None] * nvec + jnp.einsum('bhs,bhsd->bhd', wk, kc)
        return (cmat, nvec, m_new), h

    state, hs = lax.scan(step, state, xs)
    return from_chunks(hs), state


def gla_scan(q, k, v, la, state):
    xs = tuple(to_chunks(a.astype(jnp.float32)) for a in (q, k, v, la))
    tril = jnp.tril(jnp.ones((CHUNK, CHUNK), dtype=bool))

    def step(smat, inp):
        qc, kc, vc, ac = inp
        bcum = jnp.cumsum(ac, axis=2)
        qe = qc * jnp.exp(bcum)
        ke = kc * jnp.exp(-bcum)
        a = jnp.where(tril, jnp.einsum('bhtd,bhsd->bhts', qe, ke), 0.0)
        o = jnp.einsum('bhts,bhse->bhte', a, vc) + jnp.einsum('bhtd,bhde->bhte', qe, smat)
        bl = bcum[:, :, -1]
        kd = kc * jnp.exp(bl[:, :, None] - bcum)
        smat = jnp.exp(bl)[..., None] * smat + jnp.einsum('bhsd,bhse->bhde', kd, vc)
        return smat, o

    state, os_ = lax.scan(step, state, xs)
    return from_chunks(os_), state


def bidir_scan(scan, qkv_c, qkv_x, gc_f, gc_b, gx_f, gx_b, st0, need_ctx):
    def rev(arrs):
        return tuple(jnp.flip(a, axis=1) for a in arrs)
    yc_f, s_f = scan(*qkv_c, *gc_f, st0)
    yc_b, s_b = scan(*rev(qkv_c), *rev(gc_b), st0)
    yx_f, _ = scan(*qkv_x, *gx_f, s_f)
    yx_b, _ = scan(*rev(qkv_x), *rev(gx_b), s_b)
    yx = yx_f + jnp.flip(yx_b, axis=1)
    yc = yc_f + jnp.flip(yc_b, axis=1) if need_ctx else None
    return yx, yc


def even_mixer(hx, hc, w_in, w_out, diff_lambda, diff_norm_g, conv_w, conv_b, gate_b, ml_norm_g,
               lam_init, rope, need_ctx):
    dt = hx.dtype
    px = _split(hx @ w_in, EVEN_SPLITS)
    pc = _split(hc @ w_in, EVEN_SPLITS)
    lp = diff_lambda.astype(jnp.float32)
    lam = jnp.exp(jnp.sum(lp[0] * lp[1])) - jnp.exp(jnp.sum(lp[2] * lp[3])) + lam_init

    def da_heads(p):
        b, t = p[0].shape[:2]
        return (p[0].reshape(b, t, DA_HEADS, 2, DA_HD), p[1].reshape(b, t, DA_HEADS, 2, DA_HD),
                p[2].reshape(b, t, DA_HEADS, DA_VD))

    def ml_heads(p):
        b, t = p[0].shape[:2]
        qk = jax.nn.silu(dw_conv(p[3], conv_w, conv_b))
        q, k = jnp.split(qk, 2, axis=-1)
        q = q.reshape(b, t, ML_HEADS, ML_QK) * (ML_QK ** -0.5)
        k = k.reshape(b, t, ML_HEADS, ML_QK)
        v = p[4].reshape(b, t, ML_HEADS, ML_V)
        g = (p[6] + gate_b.reshape(-1)).astype(jnp.float32).reshape(b, t, 4, ML_HEADS)
        fwd = (g[:, :, 0], jax.nn.log_sigmoid(g[:, :, 2]))
        bwd = (g[:, :, 1], jax.nn.log_sigmoid(g[:, :, 3]))
        return (q, k, v), fwd, bwd

    def merge(attn, mem, o_pre):
        b, t = attn.shape[:2]
        a = (rms_norm(attn, diff_norm_g) * (1.0 - lam_init)).reshape(b, t, -1)
        m = rms_norm(mem, ml_norm_g).reshape(b, t, -1) * jax.nn.sigmoid(o_pre)
        return jnp.concatenate([a, m], axis=-1).astype(dt) @ w_out

    qx, kx, vx = da_heads(px)
    qc, kc, vc = da_heads(pc)
    qx = rope_2d(qx, rope)
    kx = rope_2d(kx, rope)
    ax = diff_attn_latent(qx, jnp.concatenate([kc, kx], axis=1), jnp.concatenate([vc, vx], axis=1), lam)

    qkv_x, gx_f, gx_b = ml_heads(px)
    qkv_c, gc_f, gc_b = ml_heads(pc)
    bsz = hx.shape[0]
    st0 = (jnp.zeros((bsz, ML_HEADS, ML_V, ML_QK), jnp.float32),
           jnp.zeros((bsz, ML_HEADS, ML_QK), jnp.float32),
           jnp.zeros((bsz, ML_HEADS), jnp.float32))
    mx, mc = bidir_scan(mlstm_scan, qkv_c, qkv_x, gc_f, gc_b, gx_f, gx_b, st0, need_ctx)

    out_x = merge(ax, mx, px[5])
    out_c = merge(diff_softmax_core(qc, kc, vc, lam), mc, pc[5]) if need_ctx else None
    return out_x, out_c


def odd_mixer(hx, hc, w_in, w_out, w_gate, b_gate, norm_g, need_ctx):
    dt = hx.dtype
    px = _split(hx @ w_in, ODD_SPLITS)
    pc = _split(hc @ w_in, ODD_SPLITS)

    def heads(p):
        b, t = p[0].shape[:2]
        q = p[0].reshape(b, t, GLA_HEADS, GLA_K) * (GLA_K ** -0.5)
        k = p[1].reshape(b, t, GLA_HEADS, GLA_K)
        v = p[2].reshape(b, t, GLA_HEADS, GLA_V)
        lr_f, lr_b = jnp.split(p[4], 2, axis=-1)

        def decay(lr, d):
            z = (lr @ w_gate[d] + b_gate[d]).astype(jnp.float32)
            return (jax.nn.log_sigmoid(z) / GLA_TAU).reshape(b, t, GLA_HEADS, GLA_K)
        return (q, k, v), (decay(lr_f, 0),), (decay(lr_b, 1),)

    def merge(o, r):
        b, t = o.shape[:2]
        y = rms_norm(o, norm_g).reshape(b, t, -1) * jax.nn.silu(r)
        return y.astype(dt) @ w_out

    qkv_x, gx_f, gx_b = heads(px)
    qkv_c, gc_f, gc_b = heads(pc)
    st0 = jnp.zeros((hx.shape[0], GLA_HEADS, GLA_K, GLA_V), jnp.float32)
    ox, oc = bidir_scan(gla_scan, qkv_c, qkv_x, gc_f, gc_b, gx_f, gx_b, st0, need_ctx)
    out_x = merge(ox, px[3])
    out_c = merge(oc, pc[3]) if need_ctx else None
    return out_x, out_c


def setup_inputs(seed: int = 0) -> dict:
    key = jax.random.key(seed)
    ks = iter(jax.random.split(key, 32))

    def nrm(shape, scale):
        return jax.random.normal(next(ks), shape, jnp.float32) * scale

    d = D_MODEL
    inv = d ** -0.5
    x = nrm((BATCH, SEQ, d), 1.0)
    c = nrm((BATCH, d), 1.0)
    ctx = nrm((BATCH, CTX_LEN, d), 1.0)
    c_ctx = nrm((d,), 1.0)
    ada_w = nrm((DEPTH, d, N_MOD * d), 0.3 * inv)
    ada_b = nrm((DEPTH, N_MOD * d), 0.02)
    norm_g = 1.0 + nrm((DEPTH, 3, d), 0.02)
    ffn_w_in = nrm((DEPTH, 2, d, 2 * D_FF), inv)
    ffn_w_out = nrm((DEPTH, 2, D_FF, d), D_FF ** -0.5)
    even_w_in = nrm((N_EVEN, d, EVEN_IN), inv)
    even_w_out = nrm((N_EVEN, d, d), inv)
    diff_lambda = nrm((N_EVEN, 4, DA_HD), 0.1)
    diff_norm_g = 1.0 + nrm((N_EVEN, DA_VD), 0.02)
    mlstm_conv_w = nrm((N_EVEN, ML_CONV, 2 * ML_HEADS * ML_QK), ML_CONV ** -0.5)
    mlstm_conv_b = nrm((N_EVEN, 2 * ML_HEADS * ML_QK), 0.02)
    fbias = jnp.concatenate([jnp.zeros((2, ML_HEADS), jnp.float32),
                             jnp.tile(jnp.linspace(3.0, 6.0, ML_HEADS, dtype=jnp.float32)[None], (2, 1))], axis=0)
    mlstm_gate_b = fbias[None] + nrm((N_EVEN, 4, ML_HEADS), 0.1)
    mlstm_norm_g = 1.0 + nrm((N_EVEN, ML_HEADS, ML_V), 0.02)
    odd_w_in = nrm((N_ODD, d, ODD_IN), inv)
    odd_w_out = nrm((N_ODD, d, d), inv)
    gla_w_gate = nrm((N_ODD, 2, GLA_RANK, GLA_HEADS * GLA_K), GLA_RANK ** -0.5)
    gla_b_gate = nrm((N_ODD, 2, GLA_HEADS * GLA_K), 0.1)
    gla_norm_g = 1.0 + nrm((N_ODD, GLA_V), 0.02)
    final_g = 1.0 + nrm((d,), 0.02)
    return {"x": x, "c": c, "ctx": ctx, "c_ctx": c_ctx, "ada_w": ada_w, "ada_b": ada_b,
            "norm_g": norm_g, "ffn_w_in": ffn_w_in, "ffn_w_out": ffn_w_out,
            "even_w_in": even_w_in, "even_w_out": even_w_out, "diff_lambda": diff_lambda,
            "diff_norm_g": diff_norm_g, "mlstm_conv_w": mlstm_conv_w, "mlstm_conv_b": mlstm_conv_b,
            "mlstm_gate_b": mlstm_gate_b, "mlstm_norm_g": mlstm_norm_g, "odd_w_in": odd_w_in,
            "odd_w_out": odd_w_out, "gla_w_gate": gla_w_gate, "gla_b_gate": gla_b_gate,
            "gla_norm_g": gla_norm_g, "final_g": final_g}


def reference(x, c, ctx, c_ctx, ada_w, ada_b, norm_g, ffn_w_in, ffn_w_out, even_w_in, even_w_out,
              diff_lambda, diff_norm_g, mlstm_conv_w, mlstm_conv_b, mlstm_gate_b, mlstm_norm_g,
              odd_w_in, odd_w_out, gla_w_gate, gla_b_gate, gla_norm_g, final_g):
    bsz, n_tok, d = x.shape
    rows = n_tok // GRID_W
    rope = axial_rope_tables(rows, x.dtype)
    s_lat = jax.nn.silu(c)
    s_ctx = jax.nn.silu(c_ctx)
    hctx = ctx
    for l in range(DEPTH):
        last = l == DEPTH - 1
        mx = (s_lat @ ada_w[l] + ada_b[l]).reshape(bsz, N_MOD, 1, d)
        mc = (s_ctx @ ada_w[l] + ada_b[l]).reshape(N_MOD, d)
        x = x + 0.5 * mx[:, 2] * swiglu(ada_norm(x, norm_g[l, 0], mx[:, 0], mx[:, 1]), ffn_w_in[l, 0], ffn_w_out[l, 0])
        hctx = hctx + 0.5 * mc[2] * swiglu(ada_norm(hctx, norm_g[l, 0], mc[0], mc[1]), ffn_w_in[l, 0], ffn_w_out[l, 0])
        hx = ada_norm(x, norm_g[l, 1], mx[:, 3], mx[:, 4])
        hc = ada_norm(hctx, norm_g[l, 1], mc[3], mc[4])
        if l % 2 == 0:
            e = l // 2
            lam_init = 0.8 - 0.6 * math.exp(-0.3 * l)
            yx, yc = even_mixer(hx, hc, even_w_in[e], even_w_out[e], diff_lambda[e], diff_norm_g[e],
                                mlstm_conv_w[e], mlstm_conv_b[e], mlstm_gate_b[e], mlstm_norm_g[e],
                                lam_init, rope, not last)
        else:
            o = l // 2
            yx, yc = odd_mixer(hx, hc, odd_w_in[o], odd_w_out[o], gla_w_gate[o], gla_b_gate[o],
                               gla_norm_g[o], not last)
        x = x + mx[:, 5] * yx
        x = x + 0.5 * mx[:, 8] * swiglu(ada_norm(x, norm_g[l, 2], mx[:, 6], mx[:, 7]), ffn_w_in[l, 1], ffn_w_out[l, 1])
        if not last:
            hctx = hctx + mc[5] * yc
            hctx = hctx + 0.5 * mc[8] * swiglu(ada_norm(hctx, norm_g[l, 2], mc[6], mc[7]), ffn_w_in[l, 1], ffn_w_out[l, 1])
    return rms_norm(x, final_g)
```

```python
import functools
import math

import jax
import jax.numpy as jnp
import numpy as np
from jax import lax
from jax.experimental import pallas as pl
from jax.experimental.pallas import tpu as pltpu

F32 = jnp.float32
BF16 = jnp.bfloat16

EPS = 1e-6
N_MOD = 9
GRID_W = 64
ROPE_BASE = 10000.0
DA_HEADS, DA_HD, DA_VD = 4, 64, 128
ML_HEADS, ML_QK, ML_V = 4, 64, 128
GLA_HEADS, GLA_K, GLA_V, GLA_RANK, GLA_TAU = 4, 128, 256, 16, 16.0

LANES = 128
MXU_DIM = 256
VMEM_LIMIT = 56 * 1024 * 1024
NEG = -1e30

ROW_TILE = 1024
FF_TILE = 256
ATT_TQ = 256
ATT_TK = 256
ML_CHUNK = 256
GLA_CHUNK = 128
CONV_TILE = 256


def _cparams(sem):
    return pltpu.CompilerParams(dimension_semantics=sem, vmem_limit_bytes=VMEM_LIMIT)


def _sigmoid(x):
    return 1.0 / (1.0 + jnp.exp(-x))


def _silu(x):
    return x * _sigmoid(x)


def _log_sigmoid(x):
    return jnp.minimum(x, 0.0) - jnp.log1p(jnp.exp(-jnp.abs(x)))


def _rms(x, axis=-1):
    return x * lax.rsqrt(jnp.mean(x * x, axis=axis, keepdims=True) + EPS)


def _ada_norm(x, g, shift, scale):
    return _rms(x) * g * (1.0 + scale) + shift


def _dot(a, b):
    return jnp.dot(a, b, preferred_element_type=F32)


def _dot_nt(a, b):
    return lax.dot_general(a, b, (((1,), (1,)), ((), ())), preferred_element_type=F32)


def _dot_tn(a, b):
    return lax.dot_general(a, b, (((0,), (0,)), ((), ())), preferred_element_type=F32)


def _cumsum_rows(x, reverse):
    n = x.shape[0]
    rows = lax.broadcasted_iota(jnp.int32, x.shape, 0)
    k = 1
    while k < n:
        if reverse:
            x = x + jnp.where(rows < n - k, pltpu.roll(x, n - k, 0), 0.0)
        else:
            x = x + jnp.where(rows >= k, pltpu.roll(x, k, 0), 0.0)
        k *= 2
    return x


def _mod_kernel(s_ref, w_ref, b_ref, o_ref):
    s = _silu(s_ref[...])
    o_ref[0] = jnp.dot(s, w_ref[0], preferred_element_type=F32,
                       precision=lax.Precision.HIGHEST) + b_ref[0]


def _modulation(svec, ada_w, ada_b):
    depth, d, n = ada_w.shape
    tn = d
    return pl.pallas_call(
        _mod_kernel,
        grid=(depth, n // tn),
        in_specs=[pl.BlockSpec(svec.shape, lambda l, j: (0, 0)),
                  pl.BlockSpec((1, d, tn), lambda l, j: (l, 0, j)),
                  pl.BlockSpec((1, 1, tn), lambda l, j: (l, 0, j))],
        out_specs=pl.BlockSpec((1, svec.shape[0], tn), lambda l, j: (l, 0, j)),
        out_shape=jax.ShapeDtypeStruct((depth, svec.shape[0], n), F32),
        compiler_params=_cparams(("parallel", "parallel")),
    )(svec, ada_w, ada_b.reshape(depth, 1, n))


def _ffn_kernel(x_ref, mod_ref, g_ref, wa_ref, wb_ref, wo_ref, fg_ref, o_ref, h_ref, acc_ref, *, base, final):
    j = pl.program_id(1)

    @pl.when(j == 0)
    def _():
        h = _ada_norm(x_ref[...], g_ref[...], mod_ref[0, base:base + 1, :], mod_ref[0, base + 1:base + 2, :])
        h_ref[...] = h.astype(BF16)
        acc_ref[...] = jnp.zeros_like(acc_ref)

    h = h_ref[...]
    u = _silu(_dot(h, wa_ref[...])) * _dot(h, wb_ref[...])
    acc_ref[...] += _dot(u.astype(BF16), wo_ref[...])

    @pl.when(j == pl.num_programs(1) - 1)
    def _():
        y = x_ref[...] + 0.5 * mod_ref[0, base + 2:base + 3, :] * acc_ref[...]
        if final:
            y = _rms(y) * fg_ref[...]
        o_ref[...] = y


def _ffn(xs, mod, g, w_in, w_out, final_g, *, lay, n_tiles, base, final):
    d = xs.shape[1]
    dff = w_out.shape[0]
    tm, tf = lay["tm"], FF_TILE
    nff = dff // tf
    midx = lay["mod_index"]
    return pl.pallas_call(
        functools.partial(_ffn_kernel, base=base, final=final),
        grid=(n_tiles, nff),
        in_specs=[pl.BlockSpec((tm, d), lambda i, j: (i, 0)),
                  pl.BlockSpec((1, N_MOD, d), lambda i, j: (midx(i), 0, 0)),
                  pl.BlockSpec((1, d), lambda i, j: (0, 0)),
                  pl.BlockSpec((d, tf), lambda i, j: (0, j)),
                  pl.BlockSpec((d, tf), lambda i, j: (0, j + nff)),
                  pl.BlockSpec((tf, d), lambda i, j: (j, 0)),
                  pl.BlockSpec((1, d), lambda i, j: (0, 0))],
        out_specs=pl.BlockSpec((tm, d), lambda i, j: (i, 0)),
        out_shape=jax.ShapeDtypeStruct((n_tiles * tm, d), F32),
        scratch_shapes=[pltpu.VMEM((tm, d), BF16), pltpu.VMEM((tm, d), F32)],
        compiler_params=_cparams(("parallel", "arbitrary")),
    )(xs, mod, g.reshape(1, d), w_in, w_in, w_out, final_g.reshape(1, d))


def _rope_heads(r, c, slo, shi, scale):
    outs = []
    for h in range(DA_HEADS):
        rh = r[:, h * LANES:(h + 1) * LANES]
        y = rh * c + pltpu.roll(rh, LANES - 16, 1) * slo + pltpu.roll(rh, 16, 1) * shi
        outs.append(y * scale if scale != 1.0 else y)
    return jnp.concatenate(outs, axis=1)


def _even_proj_kernel(x_ref, mod_ref, g_ref, w_ref, gb_ref, c_ref, slo_ref, shi_ref,
                      q_ref, k_ref, vt_ref, mlqk_ref, mlv_ref, opre_ref, gate_ref):
    h = _ada_norm(x_ref[...], g_ref[...], mod_ref[0, 3:4, :], mod_ref[0, 4:5, :]).astype(BF16)
    c, slo, shi = c_ref[...], slo_ref[...], shi_ref[...]
    q_ref[...] = _rope_heads(_dot(h, w_ref[:, 0:512]), c, slo, shi, DA_HD ** -0.5).astype(BF16)
    k_ref[...] = _rope_heads(_dot(h, w_ref[:, 512:1024]), c, slo, shi, 1.0).astype(BF16)
    vt = _dot(h, w_ref[:, 1024:1536]).T.astype(BF16)
    for i in range(vt_ref.shape[0]):
        vt_ref[i] = vt[:, i * ATT_TK:(i + 1) * ATT_TK]
    mlqk_ref[...] = _dot(h, w_ref[:, 1536:2048])
    mlv_ref[...] = _dot(h, w_ref[:, 2048:2560]).astype(BF16)
    opre_ref[...] = _dot(h, w_ref[:, 2560:3072])
    gate_ref[...] = _dot(h, w_ref[:, 3072:3200]) + gb_ref[...]


def _even_proj(xs, mod, g, w, gate_b, rope, *, lay):
    r, d = xs.shape
    tm = lay["tm"]
    n_tiles = r // tm
    midx, ridx = lay["mod_index"], lay["rope_index"]
    row = lambda i: (i, 0)
    const = lambda i: (0, 0)
    tab = pl.BlockSpec((tm, LANES), lambda i: (ridx(i), 0))
    return pl.pallas_call(
        _even_proj_kernel,
        grid=(n_tiles,),
        in_specs=[pl.BlockSpec((tm, d), row),
                  pl.BlockSpec((1, N_MOD, d), lambda i: (midx(i), 0, 0)),
                  pl.BlockSpec((1, d), const),
                  pl.BlockSpec(w.shape, const),
                  pl.BlockSpec((1, LANES), const),
                  tab, tab, tab],
        out_specs=[pl.BlockSpec((tm, 512), row), pl.BlockSpec((tm, 512), row),
                   pl.BlockSpec((tm // ATT_TK, 512, ATT_TK), lambda i: (i, 0, 0)),
                   pl.BlockSpec((tm, 512), row), pl.BlockSpec((tm, 512), row),
                   pl.BlockSpec((tm, 512), row), pl.BlockSpec((tm, LANES), row)],
        out_shape=[jax.ShapeDtypeStruct((r, 512), BF16), jax.ShapeDtypeStruct((r, 512), BF16),
                   jax.ShapeDtypeStruct((r // ATT_TK, 512, ATT_TK), BF16),
                   jax.ShapeDtypeStruct((r, 512), F32), jax.ShapeDtypeStruct((r, 512), BF16),
                   jax.ShapeDtypeStruct((r, 512), F32), jax.ShapeDtypeStruct((r, LANES), F32)],
        compiler_params=_cparams(("parallel",)),
    )(xs, mod, g.reshape(1, d), w, gate_b, *rope)


def _attn_kernel(lam_ref, g_ref, q_ref, kl_ref, kc_ref, vl_ref, vc_ref, o_ref, acc_ref, m_ref, l_ref,
                 *, nql, nkl, nkc, lam_init):
    t = pl.program_id(1)
    tq, tk = ATT_TQ, ATT_TK
    m_ref[...] = jnp.full(m_ref.shape, NEG, F32)
    l_ref[...] = jnp.zeros_like(l_ref)
    acc_ref[...] = jnp.zeros_like(acc_ref)

    q = q_ref[...]
    lane = lax.broadcasted_iota(jnp.int32, (tq, LANES), 1)
    qm = []
    for h in range(DA_HEADS):
        qh = q[:, h * LANES:(h + 1) * LANES]
        for c in range(2):
            qm.append(jnp.where((lane >= c * DA_HD) & (lane < (c + 1) * DA_HD), qh, jnp.zeros_like(qh)))

    def block(k_of, v_of):
        for h in range(DA_HEADS):
            kb, vb = k_of(h), v_of(h)
            for c in range(2):
                i = 2 * h + c
                s = _dot_nt(kb, qm[i])
                m_old = m_ref[i]
                m_new = jnp.maximum(m_old, jnp.max(s, axis=0, keepdims=True))
                alpha = jnp.exp(m_old - m_new)
                p = jnp.exp(s - m_new)
                l_ref[i] = alpha * l_ref[i] + jnp.sum(p, axis=0, keepdims=True)
                acc_ref[i] = alpha * acc_ref[i] + _dot(vb, p.astype(BF16))
                m_ref[i] = m_new

    for j in range(nkc):
        block(lambda h, j=j: kc_ref[j * tk:(j + 1) * tk, h * LANES:(h + 1) * LANES],
              lambda h, j=j: vc_ref[j, h * LANES:(h + 1) * LANES, :])

    @pl.when(t < nql)
    def _():
        def body(j, carry):
            r0 = pl.multiple_of(j * tk, tk)
            block(lambda h: kl_ref[pl.ds(r0, tk), h * LANES:(h + 1) * LANES],
                  lambda h: vl_ref[j, h * LANES:(h + 1) * LANES, :])
            return carry
        lax.fori_loop(0, nkl, body, 0)

    lp = lam_ref[...]
    lam = (jnp.exp(jnp.sum(lp[0:1] * lp[1:2], axis=1, keepdims=True))
           - jnp.exp(jnp.sum(lp[2:3] * lp[3:4], axis=1, keepdims=True)) + lam_init)
    g = g_ref[...] * (1.0 - lam_init)
    for h in range(DA_HEADS):
        ot = acc_ref[2 * h] / l_ref[2 * h] - lam * (acc_ref[2 * h + 1] / l_ref[2 * h + 1])
        y = _rms(ot, axis=0) * g
        o_ref[:, h * LANES:(h + 1) * LANES] = y.T.astype(BF16)


def _attention(q, k, vt, diff_lambda, diff_norm_g, *, lay, lam_init):
    bsz, seq, ctx = lay["b"], lay["seq"], lay["ctx"]
    r_lat = bsz * seq
    tq, tk = ATT_TQ, ATT_TK
    nql, nqc, nkl, nkc = seq // tq, ctx // tq, seq // tk, ctx // tk
    nlq = r_lat // tq
    nctx0 = r_lat // ctx

    def qidx(b, t):
        return (jnp.where(t < nql, b * nql + t, nlq + b * nqc + (t - nql)), 0)

    return pl.pallas_call(
        functools.partial(_attn_kernel, nql=nql, nkl=nkl, nkc=nkc, lam_init=lam_init),
        grid=(bsz, nql + nqc),
        in_specs=[pl.BlockSpec((4, DA_HD), lambda b, t: (0, 0)),
                  pl.BlockSpec((DA_VD, 1), lambda b, t: (0, 0)),
                  pl.BlockSpec((tq, 512), qidx),
                  pl.BlockSpec((seq, 512), lambda b, t: (b, 0)),
                  pl.BlockSpec((ctx, 512), lambda b, t: (nctx0 + b, 0)),
                  pl.BlockSpec((nkl, 512, tk), lambda b, t: (b, 0, 0)),
                  pl.BlockSpec((nkc, 512, tk), lambda b, t: (nctx0 + b, 0, 0))],
        out_specs=pl.BlockSpec((tq, 512), qidx),
        out_shape=jax.ShapeDtypeStruct(q.shape, BF16),
        scratch_shapes=[pltpu.VMEM((8, DA_VD, tq), F32), pltpu.VMEM((8, 1, tq), F32), pltpu.VMEM((8, 1, tq), F32)],
        compiler_params=_cparams(("parallel", "parallel")),
    )(diff_lambda, diff_norm_g.reshape(DA_VD, 1), q, k, k, vt, vt)


def _conv_kernel(x_ref, prev_ref, next_ref, w_ref, b_ref, q_ref, k_ref, *, lay):
    i = pl.program_id(0)
    tc = CONV_TILE
    nlb = lay["b"] * lay["seq"] // tc
    per_lat, per_ctx = lay["seq"] // tc, lay["ctx"] // tc
    pos = jnp.where(i < nlb, i % per_lat, (i - nlb) % per_ctx)
    per = jnp.where(i < nlb, per_lat, per_ctx)
    x = x_ref[...]
    rows = lax.broadcasted_iota(jnp.int32, x.shape, 0)
    prev = jnp.where(pos == 0, 0.0, prev_ref[7:8, :])
    nxt = jnp.where(pos == per - 1, 0.0, next_ref[0:1, :])
    up = jnp.where(rows == 0, prev, pltpu.roll(x, 1, 0))
    dn = jnp.where(rows == tc - 1, nxt, pltpu.roll(x, tc - 1, 0))
    y = _silu(w_ref[0:1, :] * up + w_ref[1:2, :] * x + w_ref[2:3, :] * dn + b_ref[...])
    q_ref[...] = (y[:, 0:256] * (ML_QK ** -0.5)).astype(BF16)
    k_ref[...] = y[:, 256:512].astype(BF16)


def _ml_conv(mlqk, conv_w, conv_b, *, lay):
    r = mlqk.shape[0]
    tc = CONV_TILE
    n = r // tc
    sub = tc // 8
    last = r // 8 - 1
    return pl.pallas_call(
        functools.partial(_conv_kernel, lay=lay),
        grid=(n,),
        in_specs=[pl.BlockSpec((tc, 512), lambda i: (i, 0)),
                  pl.BlockSpec((8, 512), lambda i: (jnp.maximum(i * sub - 1, 0), 0)),
                  pl.BlockSpec((8, 512), lambda i: (jnp.minimum((i + 1) * sub, last), 0)),
                  pl.BlockSpec((3, 512), lambda i: (0, 0)),
                  pl.BlockSpec((1, 512), lambda i: (0, 0))],
        out_specs=[pl.BlockSpec((tc, 256), lambda i: (i, 0)), pl.BlockSpec((tc, 256), lambda i: (i, 0))],
        out_shape=[jax.ShapeDtypeStruct((r, 256), BF16), jax.ShapeDtypeStruct((r, 256), BF16)],
        compiler_params=_cparams(("parallel",)),
    )(mlqk, mlqk, mlqk, conv_w, conv_b.reshape(1, 512))


def _scan_indices(lay, chunk):
    bsz, seq, ctx = lay["b"], lay["seq"], lay["ctx"]
    nc_lat, nc_ctx = seq // chunk, ctx // chunk
    nl = bsz * nc_lat

    def fwd(b, j):
        return jnp.where(j < nc_ctx, nl + b * nc_ctx + j, b * nc_lat + (j - nc_ctx))

    def bwd(b, j):
        return jnp.where(j < nc_ctx, nl + b * nc_ctx + (nc_ctx - 1 - j), b * nc_lat + (nc_lat - 1 - (j - nc_ctx)))

    return fwd, bwd, nc_ctx + nc_lat


def _mlstm_dir(q, k, v, gates, ctn_ref, m_ref, out_ref, d):
    n = q.shape[0]
    reverse = d == 1
    lf = _log_sigmoid(gates)
    bcum = _cumsum_rows(lf, reverse)
    bt = bcum.T
    gt = gates.T
    rows = lax.broadcasted_iota(jnp.int32, (n, n), 0)
    cols = lax.broadcasted_iota(jnp.int32, (n, n), 1)
    mask = (cols >= rows) if reverse else (cols <= rows)
    lane_q = lax.broadcasted_iota(jnp.int32, q.shape, 1)
    ctn = ctn_ref[d]
    ctn_b = ctn.astype(BF16)
    edge = 0 if reverse else n - 1
    wk_all = jnp.zeros(q.shape, F32)
    decays = []
    for h in range(ML_HEADS):
        ci, cf = d * ML_HEADS + h, (2 + d) * ML_HEADS + h
        bc = bcum[:, cf:cf + 1]
        ic = gates[:, ci:ci + 1]
        m_old = m_ref[d, h][:, 0:1]
        dm = jnp.where(mask, bc - (bt[cf:cf + 1, :] - gt[ci:ci + 1, :]), NEG)
        m_inter = bc + m_old
        m_t = jnp.maximum(m_inter, jnp.max(dm, axis=1, keepdims=True))
        w_inter = jnp.exp(m_inter - m_t)
        head = (lane_q >= h * ML_QK) & (lane_q < (h + 1) * ML_QK)
        qh = jnp.where(head, q, jnp.zeros_like(q))
        s = _dot_nt(qh, k) * jnp.exp(dm - m_t)
        r2 = _dot(qh, ctn_b)
        num = _dot(s.astype(BF16), v[:, h * ML_V:(h + 1) * ML_V]) + w_inter * r2[:, 0:ML_V]
        den = jnp.sum(s, axis=1, keepdims=True) + w_inter * r2[:, ML_V:ML_V + 1]
        out_ref[:, h * ML_V:(h + 1) * ML_V] = num / jnp.maximum(jnp.abs(den), jnp.exp(-m_t))
        bl = bcum[edge:edge + 1, cf:cf + 1]
        g = bl - bc + ic
        m_new = jnp.maximum(bl + m_old, jnp.max(g, axis=0, keepdims=True))
        decays.append(jnp.exp(bl + m_old - m_new))
        wk_all = wk_all + jnp.where(head, jnp.exp(g - m_new), 0.0)
        m_ref[d, h] = jnp.broadcast_to(m_new, (1, LANES))
    kw = (k.astype(F32) * wk_all).astype(BF16)
    vext = jnp.concatenate([v, jnp.ones((n, LANES), BF16)], axis=1)
    u = _dot_tn(kw, vext)
    for h in range(ML_HEADS):
        r0 = h * ML_QK
        ctn_ref[d, r0:r0 + ML_QK, 0:ML_V] = decays[h] * ctn[r0:r0 + ML_QK, 0:ML_V] + u[r0:r0 + ML_QK, h * ML_V:(h + 1) * ML_V]
        ctn_ref[d, r0:r0 + ML_QK, ML_V:2 * ML_V] = decays[h] * ctn[r0:r0 + ML_QK, ML_V:2 * ML_V] + u[r0:r0 + ML_QK, 4 * ML_V:5 * ML_V]


def _mlstm_kernel(qf_ref, kf_ref, vf_ref, gf_ref, qb_ref, kb_ref, vb_ref, gb_ref, of_ref, ob_ref, ctn_ref, m_ref):
    @pl.when(pl.program_id(1) == 0)
    def _():
        ctn_ref[...] = jnp.zeros_like(ctn_ref)
        m_ref[...] = jnp.zeros_like(m_ref)

    _mlstm_dir(qf_ref[...], kf_ref[...], vf_ref[...], gf_ref[...], ctn_ref, m_ref, of_ref, 0)
    _mlstm_dir(qb_ref[...], kb_ref[...], vb_ref[...], gb_ref[...], ctn_ref, m_ref, ob_ref, 1)


def _mlstm(q, k, v, gates, *, lay):
    n = ML_CHUNK
    fwd, bwd, steps = _scan_indices(lay, n)
    r = q.shape[0]

    def specs(idx):
        return [pl.BlockSpec((n, 256), lambda b, j: (idx(b, j), 0)),
                pl.BlockSpec((n, 256), lambda b, j: (idx(b, j), 0)),
                pl.BlockSpec((n, 512), lambda b, j: (idx(b, j), 0)),
                pl.BlockSpec((n, LANES), lambda b, j: (idx(b, j), 0))]

    return pl.pallas_call(
        _mlstm_kernel,
        grid=(lay["b"], steps),
        in_specs=specs(fwd) + specs(bwd),
        out_specs=[pl.BlockSpec((n, 512), lambda b, j: (fwd(b, j), 0)),
                   pl.BlockSpec((n, 512), lambda b, j: (bwd(b, j), 0))],
        out_shape=[jax.ShapeDtypeStruct((r, 512), F32), jax.ShapeDtypeStruct((r, 512), F32)],
        scratch_shapes=[pltpu.VMEM((2, 256, 256), F32), pltpu.VMEM((2, ML_HEADS, 1, LANES), F32)],
        compiler_params=_cparams(("parallel", "arbitrary")),
    )(q, k, v, gates, q, k, v, gates)


def _gla_dir(q, k, v, la, st_ref, out_ref, d):
    n = q.shape[0]
    reverse = d == 1
    rows = lax.broadcasted_iota(jnp.int32, (n, n), 0)
    cols = lax.broadcasted_iota(jnp.int32, (n, n), 1)
    mask = (cols >= rows) if reverse else (cols <= rows)
    edge = 0 if reverse else n - 1
    mid = n // 2
    for h in range(GLA_HEADS):
        ks = slice(h * GLA_K, (h + 1) * GLA_K)
        vs = slice(h * GLA_V, (h + 1) * GLA_V)
        bc = _cumsum_rows(la[:, ks], reverse)
        bmid = bc[mid:mid + 1, :]
        bl = bc[edge:edge + 1, :]
        e_up = jnp.exp(bc - bmid)
        e_dn = jnp.exp(bmid - bc)
        qf, kf = q[:, ks].astype(F32), k[:, ks].astype(F32)
        a = _dot_nt((qf * e_up).astype(BF16), (kf * e_dn).astype(BF16))
        a = jnp.where(mask, a, 0.0)
        st = st_ref[d, h]
        qs = (qf * (e_up * jnp.exp(bmid))).astype(BF16)
        out_ref[:, vs] = _dot(a.astype(BF16), v[:, vs]) + _dot_nt(qs, st.astype(BF16))
        kd = (kf * (e_dn * jnp.exp(bl - bmid))).astype(BF16)
        st_ref[d, h] = st * jnp.exp(bl) + _dot_tn(v[:, vs], kd)


def _gla_kernel(qf_ref, kf_ref, vf_ref, af_ref, qb_ref, kb_ref, vb_ref, ab_ref, of_ref, ob_ref, st_ref):
    @pl.when(pl.program_id(1) == 0)
    def _():
        st_ref[...] = jnp.zeros_like(st_ref)

    _gla_dir(qf_ref[...], kf_ref[...], vf_ref[...], af_ref[...], st_ref, of_ref, 0)
    _gla_dir(qb_ref[...], kb_ref[...], vb_ref[...], ab_ref[...], st_ref, ob_ref, 1)


def _gla(q, k, v, la_f, la_b, *, lay):
    n = GLA_CHUNK
    fwd, bwd, steps = _scan_indices(lay, n)
    r = q.shape[0]

    def specs(idx):
        return [pl.BlockSpec((n, 512), lambda b, j: (idx(b, j), 0)),
                pl.BlockSpec((n, 512), lambda b, j: (idx(b, j), 0)),
                pl.BlockSpec((n, 1024), lambda b, j: (idx(b, j), 0)),
                pl.BlockSpec((n, 512), lambda b, j: (idx(b, j), 0))]

    return pl.pallas_call(
        _gla_kernel,
        grid=(lay["b"], steps),
        in_specs=specs(fwd) + specs(bwd),
        out_specs=[pl.BlockSpec((n, 1024), lambda b, j: (fwd(b, j), 0)),
                   pl.BlockSpec((n, 1024), lambda b, j: (bwd(b, j), 0))],
        out_shape=[jax.ShapeDtypeStruct((r, 1024), F32), jax.ShapeDtypeStruct((r, 1024), F32)],
        scratch_shapes=[pltpu.VMEM((2, GLA_HEADS, GLA_V, GLA_K), F32)],
        compiler_params=_cparams(("parallel", "arbitrary")),
    )(q, k, v, la_f, q, k, v, la_b)


def _group_rms(x, width):
    outs = []
    for h in range(x.shape[1] // width):
        outs.append(_rms(x[:, h * width:(h + 1) * width]))
    return jnp.concatenate(outs, axis=1)


def _even_merge_kernel(x_ref, mod_ref, a_ref, mf_ref, mb_ref, op_ref, g_ref, w_ref, o_ref):
    m = _group_rms(mf_ref[...] + mb_ref[...], ML_V) * g_ref[...] * _sigmoid(op_ref[...])
    y = _dot(a_ref[...], w_ref[0:512, :]) + _dot(m.astype(BF16), w_ref[512:1024, :])
    o_ref[...] = x_ref[...] + mod_ref[0, 5:6, :] * y


def _even_merge(xs, mod, a, mf, mb, opre, ml_norm_g, w_out, *, lay, n_tiles):
    d = xs.shape[1]
    tm = lay["tm"]
    midx = lay["mod_index"]
    row = lambda i: (i, 0)
    const = lambda i: (0, 0)
    return pl.pallas_call(
        _even_merge_kernel,
        grid=(n_tiles,),
        in_specs=[pl.BlockSpec((tm, d), row),
                  pl.BlockSpec((1, N_MOD, d), lambda i: (midx(i), 0, 0)),
                  pl.BlockSpec((tm, 512), row), pl.BlockSpec((tm, 512), row),
                  pl.BlockSpec((tm, 512), row), pl.BlockSpec((tm, 512), row),
                  pl.BlockSpec((1, 512), const), pl.BlockSpec(w_out.shape, const)],
        out_specs=pl.BlockSpec((tm, d), row),
        out_shape=jax.ShapeDtypeStruct((n_tiles * tm, d), F32),
        compiler_params=_cparams(("parallel",)),
    )(xs, mod, a, mf, mb, opre, ml_norm_g.reshape(1, 512), w_out)


def _odd_merge_kernel(x_ref, mod_ref, of_ref, ob_ref, r_ref, g_ref, w_ref, o_ref):
    y = _group_rms(of_ref[...] + ob_ref[...], GLA_V) * g_ref[...] * _silu(r_ref[...])
    o_ref[...] = x_ref[...] + mod_ref[0, 5:6, :] * _dot(y.astype(BF16), w_ref[...])


def _odd_merge(xs, mod, of, ob, rgate, norm_g, w_out, *, lay, n_tiles):
    d = xs.shape[1]
    tm = lay["tm"]
    midx = lay["mod_index"]
    row = lambda i: (i, 0)
    const = lambda i: (0, 0)
    return pl.pallas_call(
        _odd_merge_kernel,
        grid=(n_tiles,),
        in_specs=[pl.BlockSpec((tm, d), row),
                  pl.BlockSpec((1, N_MOD, d), lambda i: (midx(i), 0, 0)),
                  pl.BlockSpec((tm, 1024), row), pl.BlockSpec((tm, 1024), row), pl.BlockSpec((tm, 1024), row),
                  pl.BlockSpec((1, 1024), const), pl.BlockSpec(w_out.shape, const)],
        out_specs=pl.BlockSpec((tm, d), row),
        out_shape=jax.ShapeDtypeStruct((n_tiles * tm, d), F32),
        compiler_params=_cparams(("parallel",)),
    )(xs, mod, of, ob, rgate, jnp.tile(norm_g, GLA_HEADS).reshape(1, 1024), w_out)


def _odd_proj_kernel(x_ref, mod_ref, g_ref, w_ref, wg_ref, bg_ref, q_ref, k_ref, v_ref, r_ref, laf_ref, lab_ref):
    h = _ada_norm(x_ref[...], g_ref[...], mod_ref[0, 3:4, :], mod_ref[0, 4:5, :]).astype(BF16)
    q_ref[...] = (_dot(h, w_ref[:, 0:512]) * (GLA_K ** -0.5)).astype(BF16)
    k_ref[...] = _dot(h, w_ref[:, 512:1024]).astype(BF16)
    v_ref[...] = _dot(h, w_ref[:, 1024:2048]).astype(BF16)
    r_ref[...] = _dot(h, w_ref[:, 2048:3072])
    lr = _dot(h, w_ref[:, 3072:3200]).astype(BF16)
    la = _log_sigmoid(_dot(lr, wg_ref[...]) + bg_ref[...]) * (1.0 / GLA_TAU)
    laf_ref[...] = la[:, 0:512]
    lab_ref[...] = la[:, 512:1024]


def _odd_proj(xs, mod, g, w, wg, bg, *, lay):
    r, d = xs.shape
    tm = lay["tm"]
    midx = lay["mod_index"]
    row = lambda i: (i, 0)
    const = lambda i: (0, 0)
    return pl.pallas_call(
        _odd_proj_kernel,
        grid=(r // tm,),
        in_specs=[pl.BlockSpec((tm, d), row),
                  pl.BlockSpec((1, N_MOD, d), lambda i: (midx(i), 0, 0)),
                  pl.BlockSpec((1, d), const), pl.BlockSpec(w.shape, const),
                  pl.BlockSpec(wg.shape, const), pl.BlockSpec((1, 1024), const)],
        out_specs=[pl.BlockSpec((tm, 512), row), pl.BlockSpec((tm, 512), row), pl.BlockSpec((tm, 1024), row),
                   pl.BlockSpec((tm, 1024), row), pl.BlockSpec((tm, 512), row), pl.BlockSpec((tm, 512), row)],
        out_shape=[jax.ShapeDtypeStruct((r, 512), BF16), jax.ShapeDtypeStruct((r, 512), BF16),
                   jax.ShapeDtypeStruct((r, 1024), BF16), jax.ShapeDtypeStruct((r, 1024), F32),
                   jax.ShapeDtypeStruct((r, 512), F32), jax.ShapeDtypeStruct((r, 512), F32)],
        compiler_params=_cparams(("parallel",)),
    )(xs, mod, g.reshape(1, d), w, wg, bg)


def _rope_tables(seq, tm):
    half = DA_HD // 4
    row = jnp.repeat(jnp.arange(seq // GRID_W), GRID_W).astype(F32)
    col = (jnp.arange(seq) % GRID_W).astype(F32)
    inv = ROPE_BASE ** (-jnp.arange(half, dtype=F32) * 2.0 / (2 * half))
    ar, ac = row[:, None] * inv, col[:, None] * inv
    cos64 = jnp.concatenate([jnp.cos(ar), jnp.cos(ar), jnp.cos(ac), jnp.cos(ac)], axis=1)
    sin_r, sin_c, z = jnp.sin(ar), jnp.sin(ac), jnp.zeros_like(ar)
    slo64 = jnp.concatenate([-sin_r, z, -sin_c, z], axis=1)
    shi64 = jnp.concatenate([z, sin_r, z, sin_c], axis=1)

    def full(t, fill):
        t = jnp.concatenate([t, t], axis=1)
        return jnp.concatenate([t, jnp.full((tm, LANES), fill, F32)], axis=0)

    return full(cos64, 1.0), full(slo64, 0.0), full(shi64, 0.0)


def _pad_cols(w, n):
    return jnp.pad(w, ((0, 0), (0, n - w.shape[1])))


def kernel(x, c, ctx, c_ctx, ada_w, ada_b, norm_g, ffn_w_in, ffn_w_out, even_w_in, even_w_out, diff_lambda,
           diff_norm_g, mlstm_conv_w, mlstm_conv_b, mlstm_gate_b, mlstm_norm_g, odd_w_in, odd_w_out,
           gla_w_gate, gla_b_gate, gla_norm_g, final_g):
    bsz, seq, d = x.shape
    n_ctx = ctx.shape[1]
    depth = ada_w.shape[0]
    r_lat, r_ctx = bsz * seq, bsz * n_ctx
    tm = min(ROW_TILE, seq, r_ctx)
    assert seq % tm == 0 and r_ctx % tm == 0 and tm % ATT_TK == 0
    assert seq % ML_CHUNK == 0 and n_ctx % ML_CHUNK == 0 and n_ctx % ATT_TQ == 0 and r_lat % n_ctx == 0
    n_lat_tiles, per_batch = r_lat // tm, seq // tm
    lay = {
        "b": bsz, "seq": seq, "ctx": n_ctx, "tm": tm,
        "mod_index": lambda i: jnp.where(i < n_lat_tiles, 1 + i // per_batch, 0),
        "rope_index": lambda i: jnp.where(i < n_lat_tiles, i % per_batch, per_batch),
    }
    n_all = (r_lat + r_ctx) // tm

    n_s = -(-(bsz + 1) // 8) * 8
    svec = jnp.zeros((n_s, d), F32).at[:bsz].set(c).at[bsz].set(c_ctx)
    mods = _modulation(svec, ada_w, ada_b)
    mods = jnp.concatenate([mods[:, bsz:bsz + 1], mods[:, :bsz]], axis=1).reshape(depth, bsz + 1, N_MOD, d)

    rope = _rope_tables(seq, tm)
    xs = jnp.concatenate([x.reshape(r_lat, d), ctx.reshape(r_ctx, d)], axis=0)
    w_in_b = ffn_w_in.astype(BF16)
    w_out_b = ffn_w_out.astype(BF16)

    for l in range(depth):
        last = l == depth - 1
        mod = mods[l]
        xs = _ffn(xs, mod, norm_g[l, 0], w_in_b[l, 0], w_out_b[l, 0], final_g, lay=lay, n_tiles=n_all, base=0, final=False)
        n_keep = n_lat_tiles if last else n_all
        if l % 2 == 0:
            e = l // 2
            lam_init = 0.8 - 0.6 * math.exp(-0.3 * l)
            w = _pad_cols(even_w_in[e], 3200).astype(BF16)
            gate_b = _pad_cols(mlstm_gate_b[e].reshape(1, -1), LANES)
            q, k, vt, mlqk, mlv, opre, gates = _even_proj(xs, mod, norm_g[l, 1], w, gate_b, rope, lay=lay)
            a = _attention(q, k, vt, diff_lambda[e], diff_norm_g[e], lay=lay, lam_init=lam_init)
            qm, km = _ml_conv(mlqk, mlstm_conv_w[e], mlstm_conv_b[e], lay=lay)
            mf, mb = _mlstm(qm, km, mlv, gates, lay=lay)
            xs = _even_merge(xs, mod, a, mf, mb, opre, mlstm_norm_g[e], even_w_out[e].astype(BF16), lay=lay, n_tiles=n_keep)
        else:
            o = l // 2
            w = _pad_cols(odd_w_in[o], 3200).astype(BF16)
            wg = jnp.zeros((LANES, 1024), F32)
            wg = wg.at[0:GLA_RANK, 0:512].set(gla_w_gate[o, 0]).at[GLA_RANK:2 * GLA_RANK, 512:1024].set(gla_w_gate[o, 1])
            bg = gla_b_gate[o].reshape(1, 1024)
            q, k, v, rgate, la_f, la_b = _odd_proj(xs, mod, norm_g[l, 1], w, wg.astype(BF16), bg, lay=lay)
            of, ob = _gla(q, k, v, la_f, la_b, lay=lay)
            xs = _odd_merge(xs, mod, of, ob, rgate, gla_norm_g[o], odd_w_out[o].astype(BF16), lay=lay, n_tiles=n_keep)
        xs = _ffn(xs, mod, norm_g[l, 2], w_in_b[l, 1], w_out_b[l, 1], final_g, lay=lay, n_tiles=n_keep, base=6, final=last)
    return xs[:r_lat].reshape(bsz, seq, d)
```

```python
import functools
import math

import jax
import jax.numpy as jnp
import numpy as np
from jax import lax
from jax.experimental import pallas as pl
from jax.experimental.pallas import tpu as pltpu

F32 = jnp.float32
BF16 = jnp.bfloat16

EPS = 1e-6
N_MOD = 9
GRID_W = 64
ROPE_BASE = 10000.0
DA_HEADS, DA_HD, DA_VD = 4, 64, 128
ML_HEADS, ML_QK, ML_V = 4, 64, 128
GLA_HEADS, GLA_K, GLA_V, GLA_RANK, GLA_TAU = 4, 128, 256, 16, 16.0

LANES = 128
MXU_DIM = 256
VMEM_LIMIT = 56 * 1024 * 1024
NEG = -1e30
LOG2E = math.log2(math.e)

ROW_TILE = 1024
FF_TILE = 256
ATT_TQ = 256
ATT_TK = 512
ATT_VBLK = 256
ML_CHUNK = 256
GLA_CHUNK = 128
CONV_TILE = 256


def _cparams(sem):
    return pltpu.CompilerParams(dimension_semantics=sem, vmem_limit_bytes=VMEM_LIMIT)


def _sigmoid(x):
    return 1.0 / (1.0 + jnp.exp(-x))


def _silu(x):
    return x * _sigmoid(x)


def _log_sigmoid(x):
    return jnp.minimum(x, 0.0) - jnp.log1p(jnp.exp(-jnp.abs(x)))


def _rms(x, axis=-1):
    return x * lax.rsqrt(jnp.mean(x * x, axis=axis, keepdims=True) + EPS)


def _ada_norm(x, g, shift, scale):
    return _rms(x) * g * (1.0 + scale) + shift


def _dot(a, b):
    return jnp.dot(a, b, preferred_element_type=F32)


def _dot_nt(a, b):
    return lax.dot_general(a, b, (((1,), (1,)), ((), ())), preferred_element_type=F32)


def _dot_tn(a, b):
    return lax.dot_general(a, b, (((0,), (0,)), ((), ())), preferred_element_type=F32)


def _scan_lanes(x, reverse, op, fill):
    n = x.shape[1]
    lane = lax.broadcasted_iota(jnp.int32, x.shape, 1)
    k = 1
    while k < n:
        if reverse:
            x = op(x, jnp.where(lane < n - k, pltpu.roll(x, n - k, 1), fill))
        else:
            x = op(x, jnp.where(lane >= k, pltpu.roll(x, k, 1), fill))
        k *= 2
    return x


def _tri_cumsum(tri, x):
    hi = x.astype(BF16)
    lo = (x - hi.astype(F32)).astype(BF16)
    r = _dot(tri, jnp.concatenate([hi, lo], axis=1))
    w = x.shape[1]
    return r[:, 0:w] + r[:, w:2 * w]


def _mod_kernel(s_ref, w_ref, b_ref, o_ref):
    s = _silu(s_ref[...])
    o_ref[0] = jnp.dot(s, w_ref[0], preferred_element_type=F32,
                       precision=lax.Precision.HIGHEST) + b_ref[0]


def _modulation(svec, ada_w, ada_b):
    depth, d, n = ada_w.shape
    tn = d
    return pl.pallas_call(
        _mod_kernel,
        grid=(depth, n // tn),
        in_specs=[pl.BlockSpec(svec.shape, lambda l, j: (0, 0)),
                  pl.BlockSpec((1, d, tn), lambda l, j: (l, 0, j)),
                  pl.BlockSpec((1, 1, tn), lambda l, j: (l, 0, j))],
        out_specs=pl.BlockSpec((1, svec.shape[0], tn), lambda l, j: (l, 0, j)),
        out_shape=jax.ShapeDtypeStruct((depth, svec.shape[0], n), F32),
        compiler_params=_cparams(("parallel", "parallel")),
    )(svec, ada_w, ada_b.reshape(depth, 1, n))


def _ffn_kernel(x_ref, mod_ref, g_ref, wa_ref, wb_ref, wo_ref, fg_ref, o_ref, h_ref, acc_ref, *, base, final):
    j = pl.program_id(1)

    @pl.when(j == 0)
    def _():
        h = _ada_norm(x_ref[...], g_ref[...], mod_ref[0, base:base + 1, :], mod_ref[0, base + 1:base + 2, :])
        h_ref[...] = h.astype(BF16)
        acc_ref[...] = jnp.zeros_like(acc_ref)

    h = h_ref[...]
    u = _silu(_dot(h, wa_ref[...])) * _dot(h, wb_ref[...])
    acc_ref[...] += _dot(u.astype(BF16), wo_ref[...])

    @pl.when(j == pl.num_programs(1) - 1)
    def _():
        y = x_ref[...] + 0.5 * mod_ref[0, base + 2:base + 3, :] * acc_ref[...]
        if final:
            y = _rms(y) * fg_ref[...]
        o_ref[...] = y


def _ffn(xs, mod, g, w_in, w_out, final_g, *, lay, n_tiles, base, final):
    d = xs.shape[1]
    dff = w_out.shape[0]
    tm, tf = lay["tm"], FF_TILE
    nff = dff // tf
    midx = lay["mod_index"]
    return pl.pallas_call(
        functools.partial(_ffn_kernel, base=base, final=final),
        grid=(n_tiles, nff),
        in_specs=[pl.BlockSpec((tm, d), lambda i, j: (i, 0)),
                  pl.BlockSpec((1, N_MOD, d), lambda i, j: (midx(i), 0, 0)),
                  pl.BlockSpec((1, d), lambda i, j: (0, 0)),
                  pl.BlockSpec((d, tf), lambda i, j: (0, j)),
                  pl.BlockSpec((d, tf), lambda i, j: (0, j + nff)),
                  pl.BlockSpec((tf, d), lambda i, j: (j, 0)),
                  pl.BlockSpec((1, d), lambda i, j: (0, 0))],
        out_specs=pl.BlockSpec((tm, d), lambda i, j: (i, 0)),
        out_shape=jax.ShapeDtypeStruct((n_tiles * tm, d), F32),
        scratch_shapes=[pltpu.VMEM((tm, d), BF16), pltpu.VMEM((tm, d), F32)],
        compiler_params=_cparams(("parallel", "arbitrary")),
    )(xs, mod, g.reshape(1, d), w_in, w_in, w_out, final_g.reshape(1, d))


def _rope_heads(r, c, slo, shi, scale):
    outs = []
    for h in range(DA_HEADS):
        rh = r[:, h * LANES:(h + 1) * LANES]
        y = rh * c + pltpu.roll(rh, LANES - 16, 1) * slo + pltpu.roll(rh, 16, 1) * shi
        outs.append(y * scale if scale != 1.0 else y)
    return jnp.concatenate(outs, axis=1)


def _even_proj_kernel(x_ref, mod_ref, g_ref, w_ref, gb_ref, c_ref, slo_ref, shi_ref,
                      q_ref, k_ref, vt_ref, mlqk_ref, mlv_ref, opre_ref, gate_ref):
    h = _ada_norm(x_ref[...], g_ref[...], mod_ref[0, 3:4, :], mod_ref[0, 4:5, :]).astype(BF16)
    c, slo, shi = c_ref[...], slo_ref[...], shi_ref[...]
    q_ref[...] = _rope_heads(_dot(h, w_ref[:, 0:512]), c, slo, shi, DA_HD ** -0.5 * LOG2E).astype(BF16)
    k_ref[...] = _rope_heads(_dot(h, w_ref[:, 512:1024]), c, slo, shi, 1.0).astype(BF16)
    vt = _dot(h, w_ref[:, 1024:1536]).T.astype(BF16)
    for i in range(vt_ref.shape[0]):
        vt_ref[i] = vt[:, i * ATT_VBLK:(i + 1) * ATT_VBLK]
    mlqk_ref[...] = _dot(h, w_ref[:, 1536:2048])
    mlv_ref[...] = _dot(h, w_ref[:, 2048:2560]).astype(BF16)
    opre_ref[...] = _dot(h, w_ref[:, 2560:3072])
    gate_ref[...] = _dot(h, w_ref[:, 3072:3200]) + gb_ref[...]


def _even_proj(xs, mod, g, w, gate_b, rope, *, lay):
    r, d = xs.shape
    tm = lay["tm"]
    n_tiles = r // tm
    midx, ridx = lay["mod_index"], lay["rope_index"]
    row = lambda i: (i, 0)
    const = lambda i: (0, 0)
    tab = pl.BlockSpec((tm, LANES), lambda i: (ridx(i), 0))
    return pl.pallas_call(
        _even_proj_kernel,
        grid=(n_tiles,),
        in_specs=[pl.BlockSpec((tm, d), row),
                  pl.BlockSpec((1, N_MOD, d), lambda i: (midx(i), 0, 0)),
                  pl.BlockSpec((1, d), const),
                  pl.BlockSpec(w.shape, const),
                  pl.BlockSpec((1, LANES), const),
                  tab, tab, tab],
        out_specs=[pl.BlockSpec((tm, 512), row), pl.BlockSpec((tm, 512), row),
                   pl.BlockSpec((tm // ATT_VBLK, 512, ATT_VBLK), lambda i: (i, 0, 0)),
                   pl.BlockSpec((tm, 512), row), pl.BlockSpec((tm, 512), row),
                   pl.BlockSpec((tm, 512), row), pl.BlockSpec((tm, LANES), row)],
        out_shape=[jax.ShapeDtypeStruct((r, 512), BF16), jax.ShapeDtypeStruct((r, 512), BF16),
                   jax.ShapeDtypeStruct((r // ATT_VBLK, 512, ATT_VBLK), BF16),
                   jax.ShapeDtypeStruct((r, 512), F32), jax.ShapeDtypeStruct((r, 512), BF16),
                   jax.ShapeDtypeStruct((r, 512), F32), jax.ShapeDtypeStruct((r, LANES), F32)],
        compiler_params=_cparams(("parallel",)),
    )(xs, mod, g.reshape(1, d), w, gate_b, *rope)


def _attn_kernel(lam_ref, g_ref, q_ref, kl_ref, kc_ref, vl_ref, vc_ref, o_ref, acc_ref, m_ref, l_ref,
                 *, nql, n_lat_chunks, n_ctx_blocks, lam_init):
    t = pl.program_id(1)
    tq, vb = ATT_TQ, ATT_VBLK
    per = ATT_TK // vb
    m_ref[...] = jnp.full(m_ref.shape, NEG, F32)
    l_ref[...] = jnp.zeros_like(l_ref)
    acc_ref[...] = jnp.zeros_like(acc_ref)

    q = q_ref[...]
    lane = lax.broadcasted_iota(jnp.int32, (tq, LANES), 1)
    qcat = []
    for h in range(DA_HEADS):
        qh = q[:, h * LANES:(h + 1) * LANES]
        qcat.append(jnp.concatenate([jnp.where(lane < DA_HD, qh, jnp.zeros_like(qh)),
                                     jnp.where(lane >= DA_HD, qh, jnp.zeros_like(qh))], axis=0))

    def block(k_of, v_of):
        ss = [_dot_nt(k_of(h), qcat[h]) for h in range(DA_HEADS)]
        alphas, ps = [], []
        for h in range(DA_HEADS):
            m_old = m_ref[h]
            m_new = jnp.maximum(m_old, jnp.max(ss[h], axis=0, keepdims=True))
            alpha = jnp.exp2(m_old - m_new)
            p = jnp.exp2(ss[h] - m_new)
            l_ref[h] = alpha * l_ref[h] + jnp.sum(p, axis=0, keepdims=True)
            m_ref[h] = m_new
            alphas.append(alpha)
            ps.append(p.astype(BF16))
        for h in range(DA_HEADS):
            acc_ref[h] = alphas[h] * acc_ref[h] + _dot(v_of(h), ps[h])

    for j in range(n_ctx_blocks):
        block(lambda h, j=j: kc_ref[j * vb:(j + 1) * vb, h * LANES:(h + 1) * LANES],
              lambda h, j=j: vc_ref[j, h * LANES:(h + 1) * LANES, :])

    @pl.when(t < nql)
    def _():
        def body(j, carry):
            r0 = pl.multiple_of(j * ATT_TK, ATT_TK)
            block(lambda h: kl_ref[pl.ds(r0, ATT_TK), h * LANES:(h + 1) * LANES],
                  lambda h: jnp.concatenate([vl_ref[j * per + i, h * LANES:(h + 1) * LANES, :] for i in range(per)], axis=1))
            return carry
        lax.fori_loop(0, n_lat_chunks, body, 0)

    lp = lam_ref[...]
    lam = (jnp.exp(jnp.sum(lp[0:1] * lp[1:2], axis=1, keepdims=True))
           - jnp.exp(jnp.sum(lp[2:3] * lp[3:4], axis=1, keepdims=True)) + lam_init)
    g = g_ref[...] * (1.0 - lam_init)
    for h in range(DA_HEADS):
        on = acc_ref[h] / l_ref[h]
        y = _rms(on[:, 0:tq] - lam * on[:, tq:2 * tq], axis=0) * g
        o_ref[:, h * LANES:(h + 1) * LANES] = y.T.astype(BF16)


def _attention(q, k, vt, diff_lambda, diff_norm_g, *, lay, lam_init):
    bsz, seq, ctx = lay["b"], lay["seq"], lay["ctx"]
    r_lat = bsz * seq
    tq, vb = ATT_TQ, ATT_VBLK
    nql, nqc, nkl, nkc = seq // tq, ctx // tq, seq // vb, ctx // vb
    nlq = r_lat // tq
    nctx0 = r_lat // ctx

    def qidx(b, t):
        return (jnp.where(t < nql, b * nql + t, nlq + b * nqc + (t - nql)), 0)

    return pl.pallas_call(
        functools.partial(_attn_kernel, nql=nql, n_lat_chunks=seq // ATT_TK, n_ctx_blocks=nkc, lam_init=lam_init),
        grid=(bsz, nql + nqc),
        in_specs=[pl.BlockSpec((4, DA_HD), lambda b, t: (0, 0)),
                  pl.BlockSpec((DA_VD, 1), lambda b, t: (0, 0)),
                  pl.BlockSpec((tq, 512), qidx),
                  pl.BlockSpec((seq, 512), lambda b, t: (b, 0)),
                  pl.BlockSpec((ctx, 512), lambda b, t: (nctx0 + b, 0)),
                  pl.BlockSpec((nkl, 512, vb), lambda b, t: (b, 0, 0)),
                  pl.BlockSpec((nkc, 512, vb), lambda b, t: (nctx0 + b, 0, 0))],
        out_specs=pl.BlockSpec((tq, 512), qidx),
        out_shape=jax.ShapeDtypeStruct(q.shape, BF16),
        scratch_shapes=[pltpu.VMEM((DA_HEADS, DA_VD, 2 * tq), F32), pltpu.VMEM((DA_HEADS, 1, 2 * tq), F32),
                        pltpu.VMEM((DA_HEADS, 1, 2 * tq), F32)],
        compiler_params=_cparams(("parallel", "parallel")),
    )(diff_lambda, diff_norm_g.reshape(DA_VD, 1), q, k, k, vt, vt)


def _conv_kernel(x_ref, prev_ref, next_ref, w_ref, b_ref, q_ref, k_ref, *, lay):
    i = pl.program_id(0)
    tc = CONV_TILE
    nlb = lay["b"] * lay["seq"] // tc
    per_lat, per_ctx = lay["seq"] // tc, lay["ctx"] // tc
    pos = jnp.where(i < nlb, i % per_lat, (i - nlb) % per_ctx)
    per = jnp.where(i < nlb, per_lat, per_ctx)
    x = x_ref[...]
    rows = lax.broadcasted_iota(jnp.int32, x.shape, 0)
    prev = jnp.where(pos == 0, 0.0, prev_ref[7:8, :])
    nxt = jnp.where(pos == per - 1, 0.0, next_ref[0:1, :])
    up = jnp.where(rows == 0, prev, pltpu.roll(x, 1, 0))
    dn = jnp.where(rows == tc - 1, nxt, pltpu.roll(x, tc - 1, 0))
    y = _silu(w_ref[0:1, :] * up + w_ref[1:2, :] * x + w_ref[2:3, :] * dn + b_ref[...])
    q_ref[...] = (y[:, 0:256] * (ML_QK ** -0.5)).astype(BF16)
    k_ref[...] = y[:, 256:512].astype(BF16)


def _ml_conv(mlqk, conv_w, conv_b, *, lay):
    r = mlqk.shape[0]
    tc = CONV_TILE
    n = r // tc
    sub = tc // 8
    last = r // 8 - 1
    return pl.pallas_call(
        functools.partial(_conv_kernel, lay=lay),
        grid=(n,),
        in_specs=[pl.BlockSpec((tc, 512), lambda i: (i, 0)),
                  pl.BlockSpec((8, 512), lambda i: (jnp.maximum(i * sub - 1, 0), 0)),
                  pl.BlockSpec((8, 512), lambda i: (jnp.minimum((i + 1) * sub, last), 0)),
                  pl.BlockSpec((3, 512), lambda i: (0, 0)),
                  pl.BlockSpec((1, 512), lambda i: (0, 0))],
        out_specs=[pl.BlockSpec((tc, 256), lambda i: (i, 0)), pl.BlockSpec((tc, 256), lambda i: (i, 0))],
        out_shape=[jax.ShapeDtypeStruct((r, 256), BF16), jax.ShapeDtypeStruct((r, 256), BF16)],
        compiler_params=_cparams(("parallel",)),
    )(mlqk, mlqk, mlqk, conv_w, conv_b.reshape(1, 512))


def _scan_indices(lay, chunk):
    bsz, seq, ctx = lay["b"], lay["seq"], lay["ctx"]
    nc_lat, nc_ctx = seq // chunk, ctx // chunk
    nl = bsz * nc_lat

    def fwd(b, j):
        return jnp.where(j < nc_ctx, nl + b * nc_ctx + j, b * nc_lat + (j - nc_ctx))

    def bwd(b, j):
        return jnp.where(j < nc_ctx, nl + b * nc_ctx + (nc_ctx - 1 - j), b * nc_lat + (nc_lat - 1 - (j - nc_ctx)))

    return fwd, bwd, nc_ctx + nc_lat


def _mlstm_step(dirs, ctn_ref, m_ref):
    n = dirs[0][0].shape[0]
    rows = lax.broadcasted_iota(jnp.int32, (n, n), 0)
    cols = lax.broadcasted_iota(jnp.int32, (n, n), 1)
    lane_q = lax.broadcasted_iota(jnp.int32, (n, ML_HEADS * ML_QK), 1)
    heads = [(lane_q >= h * ML_QK) & (lane_q < (h + 1) * ML_QK) for h in range(ML_HEADS)]
    pairs = [(d, h) for d in range(2) for h in range(ML_HEADS)]
    ones = jnp.ones((n, LANES), BF16)

    pre = []
    for d, (q, k, v, gates, _) in enumerate(dirs):
        gt = gates.T
        edge = 0 if d == 1 else n - 1
        bc = _scan_lanes(_log_sigmoid(gt[8:16, :]), d == 1, jnp.add, 0.0)
        cp = gt[0:8, :] - bc
        m_old = m_ref[d][:, 0:1]
        mx = jnp.maximum(m_old, _scan_lanes(cp, d == 1, jnp.maximum, NEG))
        mx_end = mx[:, edge:edge + 1]
        m_ref[d] = jnp.broadcast_to(bc[:, edge:edge + 1] + mx_end, (8, LANES))
        stack = jnp.concatenate([mx, bc + mx, jnp.exp(cp - mx_end), jnp.zeros((LANES - 24, n), F32)], axis=0)
        pre.append((cp, m_old, jnp.exp(m_old - mx_end), stack.T, ctn_ref[d]))

    qhs, s_raw, r2 = {}, {}, {}
    for d, h in pairs:
        q = dirs[d][0]
        qhs[d, h] = jnp.where(heads[h], q, jnp.zeros_like(q))
    for d, h in pairs:
        s_raw[d, h] = _dot_nt(qhs[d, h], dirs[d][1])
    for d, h in pairs:
        r2[d, h] = _dot(qhs[d, h], pre[d][4].astype(BF16))

    p = {}
    for d, h in pairs:
        cp, _, _, colf, _ = pre[d]
        r = d * ML_HEADS + h
        mask = (cols >= rows) if d == 1 else (cols <= rows)
        mxb = jnp.broadcast_to(colf[:, r:r + 1], (n, LANES))
        dm = jnp.where(mask, cp[r:r + 1, :] - jnp.concatenate([mxb] * (n // LANES), axis=1), NEG)
        p[d, h] = (s_raw[d, h] * jnp.exp(dm)).astype(BF16)

    nd = {}
    for d, h in pairs:
        v = dirs[d][2]
        nd[d, h] = _dot(p[d, h], jnp.concatenate([v[:, h * ML_V:(h + 1) * ML_V], ones], axis=1))

    for d, h in pairs:
        _, m_old, _, colf, _ = pre[d]
        r = d * ML_HEADS + h
        w = jnp.exp(m_old[r:r + 1, :] - jnp.broadcast_to(colf[:, r:r + 1], (n, LANES)))
        floor = jnp.exp(-jnp.broadcast_to(colf[:, 8 + r:9 + r], (n, LANES)))
        tot = nd[d, h] + jnp.concatenate([w, w], axis=1) * r2[d, h]
        dirs[d][4][:, h * ML_V:(h + 1) * ML_V] = tot[:, 0:ML_V] / jnp.maximum(jnp.abs(tot[:, ML_V:2 * ML_V]), floor)

    for d, (q, k, v, gates, _) in enumerate(dirs):
        _, _, decay, colf, ctn = pre[d]
        wk = jnp.zeros((n, ML_HEADS * ML_QK), F32)
        for h in range(ML_HEADS):
            r = d * ML_HEADS + h
            wk = jnp.where(heads[h], colf[:, 16 + r:17 + r], wk)
        kw = (k.astype(F32) * wk).astype(BF16)
        u = _dot_tn(kw, jnp.concatenate([v, ones], axis=1))
        for h in range(ML_HEADS):
            r0, r = h * ML_QK, d * ML_HEADS + h
            dec = decay[r:r + 1, :]
            ctn_ref[d, r0:r0 + ML_QK, 0:ML_V] = dec * ctn[r0:r0 + ML_QK, 0:ML_V] + u[r0:r0 + ML_QK, h * ML_V:(h + 1) * ML_V]
            ctn_ref[d, r0:r0 + ML_QK, ML_V:2 * ML_V] = (dec * ctn[r0:r0 + ML_QK, ML_V:2 * ML_V]
                                                        + u[r0:r0 + ML_QK, 4 * ML_V:5 * ML_V])


def _mlstm_kernel(qf_ref, kf_ref, vf_ref, gf_ref, qb_ref, kb_ref, vb_ref, gb_ref, of_ref, ob_ref, ctn_ref, m_ref):
    @pl.when(pl.program_id(1) == 0)
    def _():
        ctn_ref[...] = jnp.zeros_like(ctn_ref)
        m_ref[...] = jnp.zeros_like(m_ref)

    _mlstm_step([(qf_ref[...], kf_ref[...], vf_ref[...], gf_ref[...], of_ref),
                 (qb_ref[...], kb_ref[...], vb_ref[...], gb_ref[...], ob_ref)], ctn_ref, m_ref)


def _mlstm(q, k, v, gates, *, lay):
    n = ML_CHUNK
    fwd, bwd, steps = _scan_indices(lay, n)
    r = q.shape[0]

    def specs(idx):
        return [pl.BlockSpec((n, 256), lambda b, j: (idx(b, j), 0)),
                pl.BlockSpec((n, 256), lambda b, j: (idx(b, j), 0)),
                pl.BlockSpec((n, 512), lambda b, j: (idx(b, j), 0)),
                pl.BlockSpec((n, LANES), lambda b, j: (idx(b, j), 0))]

    return pl.pallas_call(
        _mlstm_kernel,
        grid=(lay["b"], steps),
        in_specs=specs(fwd) + specs(bwd),
        out_specs=[pl.BlockSpec((n, 512), lambda b, j: (fwd(b, j), 0)),
                   pl.BlockSpec((n, 512), lambda b, j: (bwd(b, j), 0))],
        out_shape=[jax.ShapeDtypeStruct((r, 512), F32), jax.ShapeDtypeStruct((r, 512), F32)],
        scratch_shapes=[pltpu.VMEM((2, 256, 256), F32), pltpu.VMEM((2, 8, LANES), F32)],
        compiler_params=_cparams(("parallel", "arbitrary")),
    )(q, k, v, gates, q, k, v, gates)


def _gla_step(dirs, st_ref):
    n = dirs[0][0].shape[0]
    rows = lax.broadcasted_iota(jnp.int32, (n, n), 0)
    cols = lax.broadcasted_iota(jnp.int32, (n, n), 1)
    mid = n // 2
    pairs = [(d, h) for d in range(2) for h in range(GLA_HEADS)]
    ks = [slice(h * GLA_K, (h + 1) * GLA_K) for h in range(GLA_HEADS)]
    vs = [slice(h * GLA_V, (h + 1) * GLA_V) for h in range(GLA_HEADS)]

    masks = [cols <= rows, cols >= rows]
    bcs = [_tri_cumsum(jnp.where(masks[d], 1.0, 0.0).astype(BF16), dirs[d][3]) for d in range(2)]

    qe, ke, qs, kd, dec, sts = {}, {}, {}, {}, {}, {}
    for d, h in pairs:
        q, k, _, la, _ = dirs[d]
        edge = 0 if d == 1 else n - 1
        bc = bcs[d][:, ks[h]]
        bmid, bl = bc[mid:mid + 1, :], bc[edge:edge + 1, :]
        e_up, e_dn = jnp.exp(bc - bmid), jnp.exp(bmid - bc)
        qf, kf = q[:, ks[h]].astype(F32), k[:, ks[h]].astype(F32)
        qe[d, h] = (qf * e_up).astype(BF16)
        ke[d, h] = (kf * e_dn).astype(BF16)
        qs[d, h] = (qf * (e_up * jnp.exp(bmid))).astype(BF16)
        kd[d, h] = (kf * (e_dn * jnp.exp(bl - bmid))).astype(BF16)
        dec[d, h] = jnp.exp(bl)
        sts[d, h] = st_ref[d, h]

    a, inter = {}, {}
    for d, h in pairs:
        a[d, h] = _dot_nt(qe[d, h], ke[d, h])
    for d, h in pairs:
        inter[d, h] = _dot_nt(qs[d, h], sts[d, h].astype(BF16))
    for d, h in pairs:
        a[d, h] = jnp.where(masks[d], a[d, h], 0.0).astype(BF16)
    for d, h in pairs:
        dirs[d][4][:, vs[h]] = _dot(a[d, h], dirs[d][2][:, vs[h]]) + inter[d, h]
    for d, h in pairs:
        st_ref[d, h] = sts[d, h] * dec[d, h] + _dot_tn(dirs[d][2][:, vs[h]], kd[d, h])


def _gla_kernel(qf_ref, kf_ref, vf_ref, af_ref, qb_ref, kb_ref, vb_ref, ab_ref, of_ref, ob_ref, st_ref):
    @pl.when(pl.program_id(1) == 0)
    def _():
        st_ref[...] = jnp.zeros_like(st_ref)

    _gla_step([(qf_ref[...], kf_ref[...], vf_ref[...], af_ref[...], of_ref),
               (qb_ref[...], kb_ref[...], vb_ref[...], ab_ref[...], ob_ref)], st_ref)


def _gla(q, k, v, la_f, la_b, *, lay):
    n = GLA_CHUNK
    fwd, bwd, steps = _scan_indices(lay, n)
    r = q.shape[0]

    def specs(idx):
        return [pl.BlockSpec((n, 512), lambda b, j: (idx(b, j), 0)),
                pl.BlockSpec((n, 512), lambda b, j: (idx(b, j), 0)),
                pl.BlockSpec((n, 1024), lambda b, j: (idx(b, j), 0)),
                pl.BlockSpec((n, 512), lambda b, j: (idx(b, j), 0))]

    return pl.pallas_call(
        _gla_kernel,
        grid=(lay["b"], steps),
        in_specs=specs(fwd) + specs(bwd),
        out_specs=[pl.BlockSpec((n, 1024), lambda b, j: (fwd(b, j), 0)),
                   pl.BlockSpec((n, 1024), lambda b, j: (bwd(b, j), 0))],
        out_shape=[jax.ShapeDtypeStruct((r, 1024), F32), jax.ShapeDtypeStruct((r, 1024), F32)],
        scratch_shapes=[pltpu.VMEM((2, GLA_HEADS, GLA_V, GLA_K), F32)],
        compiler_params=_cparams(("parallel", "arbitrary")),
    )(q, k, v, la_f, q, k, v, la_b)


def _group_rms(x, width):
    outs = []
    for h in range(x.shape[1] // width):
        outs.append(_rms(x[:, h * width:(h + 1) * width]))
    return jnp.concatenate(outs, axis=1)


def _even_merge_kernel(x_ref, mod_ref, a_ref, mf_ref, mb_ref, op_ref, g_ref, w_ref, o_ref):
    m = _group_rms(mf_ref[...] + mb_ref[...], ML_V) * g_ref[...] * _sigmoid(op_ref[...])
    y = _dot(a_ref[...], w_ref[0:512, :]) + _dot(m.astype(BF16), w_ref[512:1024, :])
    o_ref[...] = x_ref[...] + mod_ref[0, 5:6, :] * y


def _even_merge(xs, mod, a, mf, mb, opre, ml_norm_g, w_out, *, lay, n_tiles):
    d = xs.shape[1]
    tm = lay["tm"]
    midx = lay["mod_index"]
    row = lambda i: (i, 0)
    const = lambda i: (0, 0)
    return pl.pallas_call(
        _even_merge_kernel,
        grid=(n_tiles,),
        in_specs=[pl.BlockSpec((tm, d), row),
                  pl.BlockSpec((1, N_MOD, d), lambda i: (midx(i), 0, 0)),
                  pl.BlockSpec((tm, 512), row), pl.BlockSpec((tm, 512), row),
                  pl.BlockSpec((tm, 512), row), pl.BlockSpec((tm, 512), row),
                  pl.BlockSpec((1, 512), const), pl.BlockSpec(w_out.shape, const)],
        out_specs=pl.BlockSpec((tm, d), row),
        out_shape=jax.ShapeDtypeStruct((n_tiles * tm, d), F32),
        compiler_params=_cparams(("parallel",)),
    )(xs, mod, a, mf, mb, opre, ml_norm_g.reshape(1, 512), w_out)


def _odd_merge_kernel(x_ref, mod_ref, of_ref, ob_ref, r_ref, g_ref, w_ref, o_ref):
    y = _group_rms(of_ref[...] + ob_ref[...], GLA_V) * g_ref[...] * _silu(r_ref[...])
    o_ref[...] = x_ref[...] + mod_ref[0, 5:6, :] * _dot(y.astype(BF16), w_ref[...])


def _odd_merge(xs, mod, of, ob, rgate, norm_g, w_out, *, lay, n_tiles):
    d = xs.shape[1]
    tm = lay["tm"]
    midx = lay["mod_index"]
    row = lambda i: (i, 0)
    const = lambda i: (0, 0)
    return pl.pallas_call(
        _odd_merge_kernel,
        grid=(n_tiles,),
        in_specs=[pl.BlockSpec((tm, d), row),
                  pl.BlockSpec((1, N_MOD, d), lambda i: (midx(i), 0, 0)),
                  pl.BlockSpec((tm, 1024), row), pl.BlockSpec((tm, 1024), row), pl.BlockSpec((tm, 1024), row),
                  pl.BlockSpec((1, 1024), const), pl.BlockSpec(w_out.shape, const)],
        out_specs=pl.BlockSpec((tm, d), row),
        out_shape=jax.ShapeDtypeStruct((n_tiles * tm, d), F32),
        compiler_params=_cparams(("parallel",)),
    )(xs, mod, of, ob, rgate, jnp.tile(norm_g, GLA_HEADS).reshape(1, 1024), w_out)


def _odd_proj_kernel(x_ref, mod_ref, g_ref, w_ref, wg_ref, bg_ref, q_ref, k_ref, v_ref, r_ref, laf_ref, lab_ref):
    h = _ada_norm(x_ref[...], g_ref[...], mod_ref[0, 3:4, :], mod_ref[0, 4:5, :]).astype(BF16)
    q_ref[...] = (_dot(h, w_ref[:, 0:512]) * (GLA_K ** -0.5)).astype(BF16)
    k_ref[...] = _dot(h, w_ref[:, 512:1024]).astype(BF16)
    v_ref[...] = _dot(h, w_ref[:, 1024:2048]).astype(BF16)
    r_ref[...] = _dot(h, w_ref[:, 2048:3072])
    lr = _dot(h, w_ref[:, 3072:3200]).astype(BF16)
    la = _log_sigmoid(_dot(lr, wg_ref[...]) + bg_ref[...]) * (1.0 / GLA_TAU)
    laf_ref[...] = la[:, 0:512]
    lab_ref[...] = la[:, 512:1024]


def _odd_proj(xs, mod, g, w, wg, bg, *, lay):
    r, d = xs.shape
    tm = lay["tm"]
    midx = lay["mod_index"]
    row = lambda i: (i, 0)
    const = lambda i: (0, 0)
    return pl.pallas_call(
        _odd_proj_kernel,
        grid=(r // tm,),
        in_specs=[pl.BlockSpec((tm, d), row),
                  pl.BlockSpec((1, N_MOD, d), lambda i: (midx(i), 0, 0)),
                  pl.BlockSpec((1, d), const), pl.BlockSpec(w.shape, const),
                  pl.BlockSpec(wg.shape, const), pl.BlockSpec((1, 1024), const)],
        out_specs=[pl.BlockSpec((tm, 512), row), pl.BlockSpec((tm, 512), row), pl.BlockSpec((tm, 1024), row),
                   pl.BlockSpec((tm, 1024), row), pl.BlockSpec((tm, 512), row), pl.BlockSpec((tm, 512), row)],
        out_shape=[jax.ShapeDtypeStruct((r, 512), BF16), jax.ShapeDtypeStruct((r, 512), BF16),
                   jax.ShapeDtypeStruct((r, 1024), BF16), jax.ShapeDtypeStruct((r, 1024), F32),
                   jax.ShapeDtypeStruct((r, 512), F32), jax.ShapeDtypeStruct((r, 512), F32)],
        compiler_params=_cparams(("parallel",)),
    )(xs, mod, g.reshape(1, d), w, wg, bg)


def _rope_tables(seq, tm):
    half = DA_HD // 4
    row = jnp.repeat(jnp.arange(seq // GRID_W), GRID_W).astype(F32)
    col = (jnp.arange(seq) % GRID_W).astype(F32)
    inv = ROPE_BASE ** (-jnp.arange(half, dtype=F32) * 2.0 / (2 * half))
    ar, ac = row[:, None] * inv, col[:, None] * inv
    cos64 = jnp.concatenate([jnp.cos(ar), jnp.cos(ar), jnp.cos(ac), jnp.cos(ac)], axis=1)
    sin_r, sin_c, z = jnp.sin(ar), jnp.sin(ac), jnp.zeros_like(ar)
    slo64 = jnp.concatenate([-sin_r, z, -sin_c, z], axis=1)
    shi64 = jnp.concatenate([z, sin_r, z, sin_c], axis=1)

    def full(t, fill):
        t = jnp.concatenate([t, t], axis=1)
        return jnp.concatenate([t, jnp.full((tm, LANES), fill, F32)], axis=0)

    return full(cos64, 1.0), full(slo64, 0.0), full(shi64, 0.0)


def _pad_cols(w, n):
    return jnp.pad(w, ((0, 0), (0, n - w.shape[1])))


def kernel(x, c, ctx, c_ctx, ada_w, ada_b, norm_g, ffn_w_in, ffn_w_out, even_w_in, even_w_out, diff_lambda,
           diff_norm_g, mlstm_conv_w, mlstm_conv_b, mlstm_gate_b, mlstm_norm_g, odd_w_in, odd_w_out,
           gla_w_gate, gla_b_gate, gla_norm_g, final_g):
    bsz, seq, d = x.shape
    n_ctx = ctx.shape[1]
    depth = ada_w.shape[0]
    r_lat, r_ctx = bsz * seq, bsz * n_ctx
    tm = min(ROW_TILE, seq, r_ctx)
    assert seq % tm == 0 and r_ctx % tm == 0 and tm % ATT_VBLK == 0 and seq % ATT_TK == 0
    assert seq % ML_CHUNK == 0 and n_ctx % ML_CHUNK == 0 and n_ctx % ATT_TQ == 0 and r_lat % n_ctx == 0
    n_lat_tiles, per_batch = r_lat // tm, seq // tm
    lay = {
        "b": bsz, "seq": seq, "ctx": n_ctx, "tm": tm,
        "mod_index": lambda i: jnp.where(i < n_lat_tiles, 1 + i // per_batch, 0),
        "rope_index": lambda i: jnp.where(i < n_lat_tiles, i % per_batch, per_batch),
    }
    n_all = (r_lat + r_ctx) // tm

    n_s = -(-(bsz + 1) // 8) * 8
    svec = jnp.zeros((n_s, d), F32).at[:bsz].set(c).at[bsz].set(c_ctx)
    mods = _modulation(svec, ada_w, ada_b)
    mods = jnp.concatenate([mods[:, bsz:bsz + 1], mods[:, :bsz]], axis=1).reshape(depth, bsz + 1, N_MOD, d)

    rope = _rope_tables(seq, tm)
    xs = jnp.concatenate([x.reshape(r_lat, d), ctx.reshape(r_ctx, d)], axis=0)
    w_in_b = ffn_w_in.astype(BF16)
    w_out_b = ffn_w_out.astype(BF16)

    for l in range(depth):
        last = l == depth - 1
        mod = mods[l]
        xs = _ffn(xs, mod, norm_g[l, 0], w_in_b[l, 0], w_out_b[l, 0], final_g, lay=lay, n_tiles=n_all, base=0, final=False)
        n_keep = n_lat_tiles if last else n_all
        if l % 2 == 0:
            e = l // 2
            lam_init = 0.8 - 0.6 * math.exp(-0.3 * l)
            w = _pad_cols(even_w_in[e], 3200).astype(BF16)
            gate_b = _pad_cols(mlstm_gate_b[e].reshape(1, -1), LANES)
            q, k, vt, mlqk, mlv, opre, gates = _even_proj(xs, mod, norm_g[l, 1], w, gate_b, rope, lay=lay)
            a = _attention(q, k, vt, diff_lambda[e], diff_norm_g[e], lay=lay, lam_init=lam_init)
            qm, km = _ml_conv(mlqk, mlstm_conv_w[e], mlstm_conv_b[e], lay=lay)
            mf, mb = _mlstm(qm, km, mlv, gates, lay=lay)
            xs = _even_merge(xs, mod, a, mf, mb, opre, mlstm_norm_g[e], even_w_out[e].astype(BF16), lay=lay, n_tiles=n_keep)
        else:
            o = l // 2
            w = _pad_cols(odd_w_in[o], 3200).astype(BF16)
            wg = jnp.zeros((LANES, 1024), F32)
            wg = wg.at[0:GLA_RANK, 0:512].set(gla_w_gate[o, 0]).at[GLA_RANK:2 * GLA_RANK, 512:1024].set(gla_w_gate[o, 1])
            bg = gla_b_gate[o].reshape(1, 1024)
            q, k, v, rgate, la_f, la_b = _odd_proj(xs, mod, norm_g[l, 1], w, wg.astype(BF16), bg, lay=lay)
            of, ob = _gla(q, k, v, la_f, la_b, lay=lay)
            xs = _odd_merge(xs, mod, of, ob, rgate, gla_norm_g[o], odd_w_out[o].astype(BF16), lay=lay, n_tiles=n_keep)
        xs = _ffn(xs, mod, norm_g[l, 2], w_in_b[l, 1], w_out_b[l, 1], final_g, lay=lay, n_tiles=n_keep, base=6, final=last)
    return xs[:r_lat].reshape(bsz, seq, d)
```

```python
import functools
import math

import jax
import jax.numpy as jnp
import numpy as np
from jax import lax
from jax.experimental import pallas as pl
from jax.experimental.pallas import tpu as pltpu

F32 = jnp.float32
BF16 = jnp.bfloat16

EPS = 1e-6
N_MOD = 9
GRID_W = 64
ROPE_BASE = 10000.0
DA_HEADS, DA_HD, DA_VD = 4, 64, 128
ML_HEADS, ML_QK, ML_V = 4, 64, 128
GLA_HEADS, GLA_K, GLA_V, GLA_RANK, GLA_TAU = 4, 128, 256, 16, 16.0

LANES = 128
MXU_DIM = 256
VMEM_LIMIT = 56 * 1024 * 1024
NEG = -1e30
LOG2E = math.log2(math.e)

ROW_TILE = 1024
FF_TILE = 256
ATT_TQ = 256
ATT_TK = 512
ATT_VBLK = 256
ML_CHUNK = 256
GLA_CHUNK = 128
GLA_BLOCK = 256
CONV_TILE = 1024


def _cparams(sem):
    return pltpu.CompilerParams(dimension_semantics=sem, vmem_limit_bytes=VMEM_LIMIT)


def _sigmoid(x):
    return 1.0 / (1.0 + jnp.exp(-x))


def _silu(x):
    return x * _sigmoid(x)


def _log_sigmoid(x):
    return jnp.minimum(x, 0.0) - jnp.log1p(jnp.exp(-jnp.abs(x)))


def _rms(x, axis=-1):
    return x * lax.rsqrt(jnp.mean(x * x, axis=axis, keepdims=True) + EPS)


def _ada_norm(x, g, shift, scale):
    return _rms(x) * g * (1.0 + scale) + shift


def _dot(a, b):
    return jnp.dot(a, b, preferred_element_type=F32)


def _dot_nt(a, b):
    return lax.dot_general(a, b, (((1,), (1,)), ((), ())), preferred_element_type=F32)


def _dot_tn(a, b):
    return lax.dot_general(a, b, (((0,), (0,)), ((), ())), preferred_element_type=F32)


def _scan_lanes(x, reverse, op, fill):
    n = x.shape[1]
    lane = lax.broadcasted_iota(jnp.int32, x.shape, 1)
    k = 1
    while k < n:
        if reverse:
            x = op(x, jnp.where(lane < n - k, pltpu.roll(x, n - k, 1), fill))
        else:
            x = op(x, jnp.where(lane >= k, pltpu.roll(x, k, 1), fill))
        k *= 2
    return x


def _tri_cumsum(tri, x):
    if x.dtype == BF16:
        return _dot(tri, x)
    hi = x.astype(BF16)
    lo = (x - hi.astype(F32)).astype(BF16)
    r = _dot(tri, jnp.concatenate([hi, lo], axis=1))
    w = x.shape[1]
    return r[:, 0:w] + r[:, w:2 * w]


def _mod_kernel(s_ref, w_ref, b_ref, o_ref):
    s = _silu(s_ref[...])
    o_ref[0] = jnp.dot(s, w_ref[0], preferred_element_type=F32,
                       precision=lax.Precision.HIGHEST) + b_ref[0]


def _modulation(svec, ada_w, ada_b):
    depth, d, n = ada_w.shape
    tn = d
    return pl.pallas_call(
        _mod_kernel,
        grid=(depth, n // tn),
        in_specs=[pl.BlockSpec(svec.shape, lambda l, j: (0, 0)),
                  pl.BlockSpec((1, d, tn), lambda l, j: (l, 0, j)),
                  pl.BlockSpec((1, 1, tn), lambda l, j: (l, 0, j))],
        out_specs=pl.BlockSpec((1, svec.shape[0], tn), lambda l, j: (l, 0, j)),
        out_shape=jax.ShapeDtypeStruct((depth, svec.shape[0], n), F32),
        compiler_params=_cparams(("parallel", "parallel")),
    )(svec, ada_w, ada_b.reshape(depth, 1, n))


def _ffn_kernel(x_ref, c_ref, mod_ref, g_ref, wi_ref, wo_ref, fg_ref, o_ref, u_ref, *, base, final, n_first):
    dff = wo_ref.shape[0]
    x = x_ref[...]
    if n_first is not None:
        x = jnp.where(pl.program_id(0) < n_first, x, c_ref[...])
    h = _ada_norm(x, g_ref[...], mod_ref[0, base:base + 1, :], mod_ref[0, base + 1:base + 2, :]).astype(BF16)
    for j in range(dff // FF_TILE):
        a = _dot(h, wi_ref[:, j * FF_TILE:(j + 1) * FF_TILE])
        b = _dot(h, wi_ref[:, dff + j * FF_TILE:dff + (j + 1) * FF_TILE])
        u_ref[:, j * FF_TILE:(j + 1) * FF_TILE] = (_silu(a) * b).astype(BF16)
    y = x + 0.5 * mod_ref[0, base + 2:base + 3, :] * _dot(u_ref[...], wo_ref[...])
    if final:
        y = _rms(y) * fg_ref[...]
    o_ref[...] = y


def _ffn(xs, mod, g, w_in, w_out, final_g, *, lay, n_tiles, base, final, tail=None):
    d = xs.shape[1]
    dff = w_out.shape[0]
    tm = lay["tm"]
    midx = lay["mod_index"]
    const = lambda i: (0, 0)
    once = pl.Buffered(1)
    if tail is None:
        n_first, tail, xmap, cspec = None, xs, (lambda i: (i, 0)), pl.BlockSpec((8, d), const)
    else:
        n_first = xs.shape[0] // tm
        xmap = lambda i: (jnp.minimum(i, n_first - 1), 0)
        cspec = pl.BlockSpec((tm, d), lambda i: (jnp.maximum(i - n_first, 0), 0))
    return pl.pallas_call(
        functools.partial(_ffn_kernel, base=base, final=final, n_first=n_first),
        grid=(n_tiles,),
        in_specs=[pl.BlockSpec((tm, d), xmap),
                  cspec,
                  pl.BlockSpec((1, N_MOD, d), lambda i: (midx(i), 0, 0)),
                  pl.BlockSpec((1, d), const),
                  pl.BlockSpec(w_in.shape, const, pipeline_mode=once),
                  pl.BlockSpec(w_out.shape, const, pipeline_mode=once),
                  pl.BlockSpec((1, d), const)],
        out_specs=pl.BlockSpec((tm, d), lambda i: (i, 0)),
        out_shape=jax.ShapeDtypeStruct((n_tiles * tm, d), F32),
        scratch_shapes=[pltpu.VMEM((tm, dff), BF16)],
        compiler_params=_cparams(("parallel",)),
    )(xs, tail, mod, g.reshape(1, d), w_in, w_out, final_g.reshape(1, d))


def _rope_heads(r, c, slo, shi, scale):
    outs = []
    for h in range(DA_HEADS):
        rh = r[:, h * LANES:(h + 1) * LANES]
        y = rh * c + pltpu.roll(rh, LANES - 16, 1) * slo + pltpu.roll(rh, 16, 1) * shi
        outs.append(y * scale if scale != 1.0 else y)
    return jnp.concatenate(outs, axis=1)


def _even_proj_kernel(x_ref, mod_ref, g_ref, w_ref, gb_ref, c_ref, slo_ref, shi_ref,
                      q_ref, k_ref, vt_ref, mlqk_ref, mlv_ref, opre_ref, gate_ref):
    h = _ada_norm(x_ref[...], g_ref[...], mod_ref[0, 3:4, :], mod_ref[0, 4:5, :]).astype(BF16)
    c, slo, shi = c_ref[...], slo_ref[...], shi_ref[...]
    q_ref[...] = _rope_heads(_dot(h, w_ref[:, 0:512]), c, slo, shi, DA_HD ** -0.5 * LOG2E).astype(BF16)
    k_ref[...] = _rope_heads(_dot(h, w_ref[:, 512:1024]), c, slo, shi, 1.0).astype(BF16)
    vt = _dot(h, w_ref[:, 1024:1536]).T.astype(BF16)
    for i in range(vt_ref.shape[0]):
        vt_ref[i] = vt[:, i * ATT_VBLK:(i + 1) * ATT_VBLK]
    mlqk_ref[...] = _dot(h, w_ref[:, 1536:2048])
    mlv_ref[...] = _dot(h, w_ref[:, 2048:2560]).astype(BF16)
    opre_ref[...] = _dot(h, w_ref[:, 2560:3072]).astype(BF16)
    gate_ref[...] = _dot(h, w_ref[:, 3072:3200]) + gb_ref[...]


def _even_proj(xs, mod, g, w, gate_b, rope, *, lay):
    r, d = xs.shape
    tm = lay["tm"]
    n_tiles = r // tm
    midx, ridx = lay["mod_index"], lay["rope_index"]
    row = lambda i: (i, 0)
    const = lambda i: (0, 0)
    tab = pl.BlockSpec((tm, LANES), lambda i: (ridx(i), 0))
    return pl.pallas_call(
        _even_proj_kernel,
        grid=(n_tiles,),
        in_specs=[pl.BlockSpec((tm, d), row),
                  pl.BlockSpec((1, N_MOD, d), lambda i: (midx(i), 0, 0)),
                  pl.BlockSpec((1, d), const),
                  pl.BlockSpec(w.shape, const),
                  pl.BlockSpec((1, LANES), const),
                  tab, tab, tab],
        out_specs=[pl.BlockSpec((tm, 512), row), pl.BlockSpec((tm, 512), row),
                   pl.BlockSpec((tm // ATT_VBLK, 512, ATT_VBLK), lambda i: (i, 0, 0)),
                   pl.BlockSpec((tm, 512), row), pl.BlockSpec((tm, 512), row),
                   pl.BlockSpec((tm, 512), row), pl.BlockSpec((tm, LANES), row)],
        out_shape=[jax.ShapeDtypeStruct((r, 512), BF16), jax.ShapeDtypeStruct((r, 512), BF16),
                   jax.ShapeDtypeStruct((r // ATT_VBLK, 512, ATT_VBLK), BF16),
                   jax.ShapeDtypeStruct((r, 512), F32), jax.ShapeDtypeStruct((r, 512), BF16),
                   jax.ShapeDtypeStruct((r, 512), BF16), jax.ShapeDtypeStruct((r, LANES), F32)],
        compiler_params=_cparams(("parallel",)),
    )(xs, mod, g.reshape(1, d), w, gate_b, *rope)


def _attn_kernel(lam_ref, g_ref, q_ref, kl_ref, kc_ref, vl_ref, vc_ref, o_ref, acc_ref, m_ref, l_ref,
                 *, nql, n_lat_chunks, n_ctx_blocks, lam_init):
    t = pl.program_id(1)
    tq, vb = ATT_TQ, ATT_VBLK
    per = ATT_TK // vb
    m_ref[...] = jnp.full(m_ref.shape, NEG, F32)
    l_ref[...] = jnp.zeros_like(l_ref)
    acc_ref[...] = jnp.zeros_like(acc_ref)

    q = q_ref[...]
    lane = lax.broadcasted_iota(jnp.int32, (tq, LANES), 1)
    qcat = []
    for h in range(DA_HEADS):
        qh = q[:, h * LANES:(h + 1) * LANES]
        qcat.append(jnp.concatenate([jnp.where(lane < DA_HD, qh, jnp.zeros_like(qh)),
                                     jnp.where(lane >= DA_HD, qh, jnp.zeros_like(qh))], axis=0))

    def block(k_of, v_of):
        ss = [_dot_nt(k_of(h), qcat[h]) for h in range(DA_HEADS)]
        alphas, ps = [], []
        for h in range(DA_HEADS):
            m_old = m_ref[h]
            m_new = jnp.maximum(m_old, jnp.max(ss[h], axis=0, keepdims=True))
            alpha = jnp.exp2(m_old - m_new)
            p = jnp.exp2(ss[h] - m_new)
            l_ref[h] = alpha * l_ref[h] + jnp.sum(p, axis=0, keepdims=True)
            m_ref[h] = m_new
            alphas.append(alpha)
            ps.append(p.astype(BF16))
        for h in range(DA_HEADS):
            acc_ref[h] = alphas[h] * acc_ref[h] + _dot(v_of(h), ps[h])

    for j in range(n_ctx_blocks):
        block(lambda h, j=j: kc_ref[j * vb:(j + 1) * vb, h * LANES:(h + 1) * LANES],
              lambda h, j=j: vc_ref[j, h * LANES:(h + 1) * LANES, :])

    @pl.when(t < nql)
    def _():
        def body(j, carry):
            r0 = pl.multiple_of(j * ATT_TK, ATT_TK)
            block(lambda h: kl_ref[pl.ds(r0, ATT_TK), h * LANES:(h + 1) * LANES],
                  lambda h: jnp.concatenate([vl_ref[j * per + i, h * LANES:(h + 1) * LANES, :] for i in range(per)], axis=1))
            return carry
        lax.fori_loop(0, n_lat_chunks, body, 0)

    lp = lam_ref[...]
    lam = (jnp.exp(jnp.sum(lp[0:1] * lp[1:2], axis=1, keepdims=True))
           - jnp.exp(jnp.sum(lp[2:3] * lp[3:4], axis=1, keepdims=True)) + lam_init)
    g = g_ref[...] * (1.0 - lam_init)
    for h in range(DA_HEADS):
        on = acc_ref[h] / l_ref[h]
        y = _rms(on[:, 0:tq] - lam * on[:, tq:2 * tq], axis=0) * g
        o_ref[:, h * LANES:(h + 1) * LANES] = y.T.astype(BF16)


def _attention(q, k, vt, diff_lambda, diff_norm_g, *, lay, lam_init):
    bsz, seq, ctx = lay["b"], lay["seq"], lay["ctx"]
    r_lat = bsz * seq
    tq, vb = ATT_TQ, ATT_VBLK
    nql, nqc, nkl, nkc = seq // tq, ctx // tq, seq // vb, ctx // vb
    nlq = r_lat // tq
    nctx0 = r_lat // ctx

    def qidx(b, t):
        return (jnp.where(t < nql, b * nql + t, nlq + b * nqc + (t - nql)), 0)

    return pl.pallas_call(
        functools.partial(_attn_kernel, nql=nql, n_lat_chunks=seq // ATT_TK, n_ctx_blocks=nkc, lam_init=lam_init),
        grid=(bsz, nql + nqc),
        in_specs=[pl.BlockSpec((4, DA_HD), lambda b, t: (0, 0)),
                  pl.BlockSpec((DA_VD, 1), lambda b, t: (0, 0)),
                  pl.BlockSpec((tq, 512), qidx),
                  pl.BlockSpec((seq, 512), lambda b, t: (b, 0)),
                  pl.BlockSpec((ctx, 512), lambda b, t: (nctx0 + b, 0)),
                  pl.BlockSpec((nkl, 512, vb), lambda b, t: (b, 0, 0)),
                  pl.BlockSpec((nkc, 512, vb), lambda b, t: (nctx0 + b, 0, 0))],
        out_specs=pl.BlockSpec((tq, 512), qidx),
        out_shape=jax.ShapeDtypeStruct(q.shape, BF16),
        scratch_shapes=[pltpu.VMEM((DA_HEADS, DA_VD, 2 * tq), F32), pltpu.VMEM((DA_HEADS, 1, 2 * tq), F32),
                        pltpu.VMEM((DA_HEADS, 1, 2 * tq), F32)],
        compiler_params=_cparams(("parallel", "parallel")),
    )(diff_lambda, diff_norm_g.reshape(DA_VD, 1), q, k, k, vt, vt)


def _conv_kernel(x_ref, prev_ref, next_ref, w_ref, b_ref, q_ref, k_ref, *, lay):
    i = pl.program_id(0)
    tc = x_ref.shape[0]
    seq_len = jnp.where(i < lay["b"] * lay["seq"] // tc, lay["seq"], lay["ctx"])
    x = x_ref[...]
    rows = lax.broadcasted_iota(jnp.int32, x.shape, 0)
    pos = (i * tc + rows) & (seq_len - 1)
    up = jnp.where(rows == 0, prev_ref[7:8, :], pltpu.roll(x, 1, 0))
    dn = jnp.where(rows == tc - 1, next_ref[0:1, :], pltpu.roll(x, tc - 1, 0))
    up = jnp.where(pos == 0, 0.0, up)
    dn = jnp.where(pos == seq_len - 1, 0.0, dn)
    y = _silu(w_ref[0:1, :] * up + w_ref[1:2, :] * x + w_ref[2:3, :] * dn + b_ref[...])
    q_ref[...] = (y[:, 0:256] * (ML_QK ** -0.5)).astype(BF16)
    k_ref[...] = y[:, 256:512].astype(BF16)


def _ml_conv(mlqk, conv_w, conv_b, *, lay):
    r = mlqk.shape[0]
    tc = min(CONV_TILE, lay["tm"])
    n = r // tc
    sub = tc // 8
    last = r // 8 - 1
    return pl.pallas_call(
        functools.partial(_conv_kernel, lay=lay),
        grid=(n,),
        in_specs=[pl.BlockSpec((tc, 512), lambda i: (i, 0)),
                  pl.BlockSpec((8, 512), lambda i: (jnp.maximum(i * sub - 1, 0), 0)),
                  pl.BlockSpec((8, 512), lambda i: (jnp.minimum((i + 1) * sub, last), 0)),
                  pl.BlockSpec((3, 512), lambda i: (0, 0)),
                  pl.BlockSpec((1, 512), lambda i: (0, 0))],
        out_specs=[pl.BlockSpec((tc, 256), lambda i: (i, 0)), pl.BlockSpec((tc, 256), lambda i: (i, 0))],
        out_shape=[jax.ShapeDtypeStruct((r, 256), BF16), jax.ShapeDtypeStruct((r, 256), BF16)],
        compiler_params=_cparams(("parallel",)),
    )(mlqk, mlqk, mlqk, conv_w, conv_b.reshape(1, 512))


def _scan_indices(lay, chunk):
    bsz, seq, ctx = lay["b"], lay["seq"], lay["ctx"]
    nc_lat, nc_ctx = seq // chunk, ctx // chunk
    nl = bsz * nc_lat

    def fwd(b, j):
        return jnp.where(j < nc_ctx, nl + b * nc_ctx + j, b * nc_lat + (j - nc_ctx))

    def bwd(b, j):
        return jnp.where(j < nc_ctx, nl + b * nc_ctx + (nc_ctx - 1 - j), b * nc_lat + (nc_lat - 1 - (j - nc_ctx)))

    return fwd, bwd, nc_ctx + nc_lat


def _mlstm_step(dirs, ctn_ref, m_ref):
    n = dirs[0][0].shape[0]
    rows = lax.broadcasted_iota(jnp.int32, (n, n), 0)
    cols = lax.broadcasted_iota(jnp.int32, (n, n), 1)
    lane_q = lax.broadcasted_iota(jnp.int32, (n, ML_HEADS * ML_QK), 1)
    heads = [(lane_q >= h * ML_QK) & (lane_q < (h + 1) * ML_QK) for h in range(ML_HEADS)]
    pairs = [(d, h) for d in range(2) for h in range(ML_HEADS)]
    ones = jnp.ones((n, LANES), BF16)

    pre = []
    for d, (q, k, v, gates, _) in enumerate(dirs):
        gt = gates.T
        edge = 0 if d == 1 else n - 1
        bc = _scan_lanes(_log_sigmoid(gt[8:16, :]), d == 1, jnp.add, 0.0)
        cp = gt[0:8, :] - bc
        m_old = m_ref[d][:, 0:1]
        mx = jnp.maximum(m_old, _scan_lanes(cp, d == 1, jnp.maximum, NEG))
        mx_end = mx[:, edge:edge + 1]
        m_ref[d] = jnp.broadcast_to(bc[:, edge:edge + 1] + mx_end, (8, LANES))
        stack = jnp.concatenate([mx, bc + mx, jnp.exp(cp - mx_end), jnp.zeros((LANES - 24, n), F32)], axis=0)
        pre.append((cp, m_old, jnp.exp(m_old - mx_end), stack.T, ctn_ref[d]))

    qhs, s_raw, r2 = {}, {}, {}
    for d, h in pairs:
        q = dirs[d][0]
        qhs[d, h] = jnp.where(heads[h], q, jnp.zeros_like(q))
    for d, h in pairs:
        s_raw[d, h] = _dot_nt(qhs[d, h], dirs[d][1])
    for d, h in pairs:
        r2[d, h] = _dot(qhs[d, h], pre[d][4].astype(BF16))

    p = {}
    for d, h in pairs:
        cp, _, _, colf, _ = pre[d]
        r = d * ML_HEADS + h
        mask = (cols >= rows) if d == 1 else (cols <= rows)
        mxb = jnp.broadcast_to(colf[:, r:r + 1], (n, LANES))
        dm = jnp.where(mask, cp[r:r + 1, :] - jnp.concatenate([mxb] * (n // LANES), axis=1), NEG)
        p[d, h] = (s_raw[d, h] * jnp.exp(dm)).astype(BF16)

    nd = {}
    for d, h in pairs:
        v = dirs[d][2]
        nd[d, h] = _dot(p[d, h], jnp.concatenate([v[:, h * ML_V:(h + 1) * ML_V], ones], axis=1))

    for d, h in pairs:
        _, m_old, _, colf, _ = pre[d]
        r = d * ML_HEADS + h
        w = jnp.exp(m_old[r:r + 1, :] - jnp.broadcast_to(colf[:, r:r + 1], (n, LANES)))
        floor = jnp.exp(-jnp.broadcast_to(colf[:, 8 + r:9 + r], (n, LANES)))
        tot = nd[d, h] + jnp.concatenate([w, w], axis=1) * r2[d, h]
        hid = tot[:, 0:ML_V] / jnp.maximum(jnp.abs(tot[:, ML_V:2 * ML_V]), floor)
        dirs[d][4][:, h * ML_V:(h + 1) * ML_V] = hid.astype(BF16)

    for d, (q, k, v, gates, _) in enumerate(dirs):
        _, _, decay, colf, ctn = pre[d]
        wk = jnp.zeros((n, ML_HEADS * ML_QK), F32)
        for h in range(ML_HEADS):
            r = d * ML_HEADS + h
            wk = jnp.where(heads[h], colf[:, 16 + r:17 + r], wk)
        kw = (k.astype(F32) * wk).astype(BF16)
        u = _dot_tn(kw, jnp.concatenate([v, ones], axis=1))
        for h in range(ML_HEADS):
            r0, r = h * ML_QK, d * ML_HEADS + h
            dec = decay[r:r + 1, :]
            ctn_ref[d, r0:r0 + ML_QK, 0:ML_V] = dec * ctn[r0:r0 + ML_QK, 0:ML_V] + u[r0:r0 + ML_QK, h * ML_V:(h + 1) * ML_V]
            ctn_ref[d, r0:r0 + ML_QK, ML_V:2 * ML_V] = (dec * ctn[r0:r0 + ML_QK, ML_V:2 * ML_V]
                                                        + u[r0:r0 + ML_QK, 4 * ML_V:5 * ML_V])


def _mlstm_kernel(qf_ref, kf_ref, vf_ref, gf_ref, qb_ref, kb_ref, vb_ref, gb_ref, of_ref, ob_ref, ctn_ref, m_ref):
    @pl.when(pl.program_id(1) == 0)
    def _():
        ctn_ref[...] = jnp.zeros_like(ctn_ref)
        m_ref[...] = jnp.zeros_like(m_ref)

    _mlstm_step([(qf_ref[...], kf_ref[...], vf_ref[...], gf_ref[...], of_ref),
                 (qb_ref[...], kb_ref[...], vb_ref[...], gb_ref[...], ob_ref)], ctn_ref, m_ref)


def _mlstm(q, k, v, gates, *, lay):
    n = ML_CHUNK
    fwd, bwd, steps = _scan_indices(lay, n)
    r = q.shape[0]

    def specs(idx):
        return [pl.BlockSpec((n, 256), lambda b, j: (idx(b, j), 0)),
                pl.BlockSpec((n, 256), lambda b, j: (idx(b, j), 0)),
                pl.BlockSpec((n, 512), lambda b, j: (idx(b, j), 0)),
                pl.BlockSpec((n, LANES), lambda b, j: (idx(b, j), 0))]

    return pl.pallas_call(
        _mlstm_kernel,
        grid=(lay["b"], steps),
        in_specs=specs(fwd) + specs(bwd),
        out_specs=[pl.BlockSpec((n, 512), lambda b, j: (fwd(b, j), 0)),
                   pl.BlockSpec((n, 512), lambda b, j: (bwd(b, j), 0))],
        out_shape=[jax.ShapeDtypeStruct((r, 512), BF16), jax.ShapeDtypeStruct((r, 512), BF16)],
        scratch_shapes=[pltpu.VMEM((2, 256, 256), F32), pltpu.VMEM((2, 8, LANES), F32)],
        compiler_params=_cparams(("parallel", "arbitrary")),
    )(q, k, v, gates, q, k, v, gates)


def _gla_step(dirs, st_ref):
    n = dirs[0][0].shape[0]
    rows = lax.broadcasted_iota(jnp.int32, (n, n), 0)
    cols = lax.broadcasted_iota(jnp.int32, (n, n), 1)
    mid = n // 2
    pairs = [(d, h) for d in range(2) for h in range(GLA_HEADS)]
    ks = [slice(h * GLA_K, (h + 1) * GLA_K) for h in range(GLA_HEADS)]
    vs = [slice(h * GLA_V, (h + 1) * GLA_V) for h in range(GLA_HEADS)]

    masks = [cols <= rows, cols >= rows]
    bcs = [_tri_cumsum(jnp.where(masks[d], 1.0, 0.0).astype(BF16), dirs[d][3]) for d in range(2)]

    qe, ke, qs, kd, dec, sts = {}, {}, {}, {}, {}, {}
    for d, h in pairs:
        q, k, _, la, _ = dirs[d]
        edge = 0 if d == 1 else n - 1
        bc = bcs[d][:, ks[h]]
        bmid, bl = bc[mid:mid + 1, :], bc[edge:edge + 1, :]
        e_up, e_dn = jnp.exp(bc - bmid), jnp.exp(bmid - bc)
        qf, kf = q[:, ks[h]].astype(F32), k[:, ks[h]].astype(F32)
        qe[d, h] = (qf * e_up).astype(BF16)
        ke[d, h] = (kf * e_dn).astype(BF16)
        qs[d, h] = (qf * (e_up * jnp.exp(bmid))).astype(BF16)
        kd[d, h] = (kf * (e_dn * jnp.exp(bl - bmid))).astype(BF16)
        dec[d, h] = jnp.exp(bl)
        sts[d, h] = st_ref[d, h]

    a, inter = {}, {}
    for d, h in pairs:
        a[d, h] = _dot_nt(qe[d, h], ke[d, h])
    for d, h in pairs:
        inter[d, h] = _dot_nt(qs[d, h], sts[d, h].astype(BF16))
    for d, h in pairs:
        a[d, h] = jnp.where(masks[d], a[d, h], 0.0).astype(BF16)
    for d, h in pairs:
        dirs[d][4][:, vs[h]] = (_dot(a[d, h], dirs[d][2][:, vs[h]]) + inter[d, h]).astype(BF16)
    for d, h in pairs:
        st_ref[d, h] = sts[d, h] * dec[d, h] + _dot_tn(dirs[d][2][:, vs[h]], kd[d, h])


def _gla_kernel(qf_ref, kf_ref, vf_ref, af_ref, qb_ref, kb_ref, vb_ref, ab_ref, of_ref, ob_ref, st_ref):
    @pl.when(pl.program_id(1) == 0)
    def _():
        st_ref[...] = jnp.zeros_like(st_ref)

    n = GLA_CHUNK
    per = qf_ref.shape[0] // n
    for c in range(per):
        f, b = pl.ds(c * n, n), pl.ds((per - 1 - c) * n, n)
        _gla_step([(qf_ref[f, :], kf_ref[f, :], vf_ref[f, :], af_ref[f, :], of_ref.at[f]),
                   (qb_ref[b, :], kb_ref[b, :], vb_ref[b, :], ab_ref[b, :], ob_ref.at[b])], st_ref)


def _gla(q, k, v, la_f, la_b, *, lay):
    n = GLA_BLOCK
    fwd, bwd, steps = _scan_indices(lay, n)
    r = q.shape[0]

    def specs(idx):
        return [pl.BlockSpec((n, 512), lambda b, j: (idx(b, j), 0)),
                pl.BlockSpec((n, 512), lambda b, j: (idx(b, j), 0)),
                pl.BlockSpec((n, 1024), lambda b, j: (idx(b, j), 0)),
                pl.BlockSpec((n, 512), lambda b, j: (idx(b, j), 0))]

    return pl.pallas_call(
        _gla_kernel,
        grid=(lay["b"], steps),
        in_specs=specs(fwd) + specs(bwd),
        out_specs=[pl.BlockSpec((n, 1024), lambda b, j: (fwd(b, j), 0)),
                   pl.BlockSpec((n, 1024), lambda b, j: (bwd(b, j), 0))],
        out_shape=[jax.ShapeDtypeStruct((r, 1024), BF16), jax.ShapeDtypeStruct((r, 1024), BF16)],
        scratch_shapes=[pltpu.VMEM((2, GLA_HEADS, GLA_V, GLA_K), F32)],
        compiler_params=_cparams(("parallel", "arbitrary")),
    )(q, k, v, la_f, q, k, v, la_b)


def _group_rms(x, width):
    outs = []
    for h in range(x.shape[1] // width):
        outs.append(_rms(x[:, h * width:(h + 1) * width]))
    return jnp.concatenate(outs, axis=1)


def _even_merge_kernel(x_ref, mod_ref, a_ref, mf_ref, mb_ref, op_ref, g_ref, w_ref, o_ref):
    mem = mf_ref[...].astype(F32) + mb_ref[...].astype(F32)
    m = _group_rms(mem, ML_V) * g_ref[...] * _sigmoid(op_ref[...].astype(F32))
    y = _dot(a_ref[...], w_ref[0:512, :]) + _dot(m.astype(BF16), w_ref[512:1024, :])
    o_ref[...] = x_ref[...] + mod_ref[0, 5:6, :] * y


def _even_merge(xs, mod, a, mf, mb, opre, ml_norm_g, w_out, *, lay, n_tiles):
    d = xs.shape[1]
    tm = lay["tm"]
    midx = lay["mod_index"]
    row = lambda i: (i, 0)
    const = lambda i: (0, 0)
    return pl.pallas_call(
        _even_merge_kernel,
        grid=(n_tiles,),
        in_specs=[pl.BlockSpec((tm, d), row),
                  pl.BlockSpec((1, N_MOD, d), lambda i: (midx(i), 0, 0)),
                  pl.BlockSpec((tm, 512), row), pl.BlockSpec((tm, 512), row),
                  pl.BlockSpec((tm, 512), row), pl.BlockSpec((tm, 512), row),
                  pl.BlockSpec((1, 512), const), pl.BlockSpec(w_out.shape, const)],
        out_specs=pl.BlockSpec((tm, d), row),
        out_shape=jax.ShapeDtypeStruct((n_tiles * tm, d), F32),
        compiler_params=_cparams(("parallel",)),
    )(xs, mod, a, mf, mb, opre, ml_norm_g.reshape(1, 512), w_out)


def _odd_merge_kernel(x_ref, mod_ref, of_ref, ob_ref, r_ref, g_ref, w_ref, o_ref):
    o = of_ref[...].astype(F32) + ob_ref[...].astype(F32)
    y = _group_rms(o, GLA_V) * g_ref[...] * _silu(r_ref[...].astype(F32))
    o_ref[...] = x_ref[...] + mod_ref[0, 5:6, :] * _dot(y.astype(BF16), w_ref[...])


def _odd_merge(xs, mod, of, ob, rgate, norm_g, w_out, *, lay, n_tiles):
    d = xs.shape[1]
    tm = lay["tm"]
    midx = lay["mod_index"]
    row = lambda i: (i, 0)
    const = lambda i: (0, 0)
    return pl.pallas_call(
        _odd_merge_kernel,
        grid=(n_tiles,),
        in_specs=[pl.BlockSpec((tm, d), row),
                  pl.BlockSpec((1, N_MOD, d), lambda i: (midx(i), 0, 0)),
                  pl.BlockSpec((tm, 1024), row), pl.BlockSpec((tm, 1024), row), pl.BlockSpec((tm, 1024), row),
                  pl.BlockSpec((1, 1024), const), pl.BlockSpec(w_out.shape, const)],
        out_specs=pl.BlockSpec((tm, d), row),
        out_shape=jax.ShapeDtypeStruct((n_tiles * tm, d), F32),
        compiler_params=_cparams(("parallel",)),
    )(xs, mod, of, ob, rgate, jnp.tile(norm_g, GLA_HEADS).reshape(1, 1024), w_out)


def _odd_proj_kernel(x_ref, mod_ref, g_ref, w_ref, wg_ref, bg_ref, q_ref, k_ref, v_ref, r_ref, laf_ref, lab_ref):
    h = _ada_norm(x_ref[...], g_ref[...], mod_ref[0, 3:4, :], mod_ref[0, 4:5, :]).astype(BF16)
    q_ref[...] = (_dot(h, w_ref[:, 0:512]) * (GLA_K ** -0.5)).astype(BF16)
    k_ref[...] = _dot(h, w_ref[:, 512:1024]).astype(BF16)
    v_ref[...] = _dot(h, w_ref[:, 1024:2048]).astype(BF16)
    r_ref[...] = _dot(h, w_ref[:, 2048:3072]).astype(BF16)
    lr = _dot(h, w_ref[:, 3072:3200]).astype(BF16)
    la = _log_sigmoid(_dot(lr, wg_ref[...]) + bg_ref[...]) * (1.0 / GLA_TAU)
    laf_ref[...] = la[:, 0:512].astype(BF16)
    lab_ref[...] = la[:, 512:1024].astype(BF16)


def _odd_proj(xs, mod, g, w, wg, bg, *, lay):
    r, d = xs.shape
    tm = lay["tm"]
    midx = lay["mod_index"]
    row = lambda i: (i, 0)
    const = lambda i: (0, 0)
    return pl.pallas_call(
        _odd_proj_kernel,
        grid=(r // tm,),
        in_specs=[pl.BlockSpec((tm, d), row),
                  pl.BlockSpec((1, N_MOD, d), lambda i: (midx(i), 0, 0)),
                  pl.BlockSpec((1, d), const), pl.BlockSpec(w.shape, const),
                  pl.BlockSpec(wg.shape, const), pl.BlockSpec((1, 1024), const)],
        out_specs=[pl.BlockSpec((tm, 512), row), pl.BlockSpec((tm, 512), row), pl.BlockSpec((tm, 1024), row),
                   pl.BlockSpec((tm, 1024), row), pl.BlockSpec((tm, 512), row), pl.BlockSpec((tm, 512), row)],
        out_shape=[jax.ShapeDtypeStruct((r, 512), BF16), jax.ShapeDtypeStruct((r, 512), BF16),
                   jax.ShapeDtypeStruct((r, 1024), BF16), jax.ShapeDtypeStruct((r, 1024), BF16),
                   jax.ShapeDtypeStruct((r, 512), BF16), jax.ShapeDtypeStruct((r, 512), BF16)],
        compiler_params=_cparams(("parallel",)),
    )(xs, mod, g.reshape(1, d), w, wg, bg)


def _rope_tables(seq, tm):
    half = DA_HD // 4
    row = jnp.repeat(jnp.arange(seq // GRID_W), GRID_W).astype(F32)
    col = (jnp.arange(seq) % GRID_W).astype(F32)
    inv = ROPE_BASE ** (-jnp.arange(half, dtype=F32) * 2.0 / (2 * half))
    ar, ac = row[:, None] * inv, col[:, None] * inv
    cos64 = jnp.concatenate([jnp.cos(ar), jnp.cos(ar), jnp.cos(ac), jnp.cos(ac)], axis=1)
    sin_r, sin_c, z = jnp.sin(ar), jnp.sin(ac), jnp.zeros_like(ar)
    slo64 = jnp.concatenate([-sin_r, z, -sin_c, z], axis=1)
    shi64 = jnp.concatenate([z, sin_r, z, sin_c], axis=1)

    def full(t, fill):
        t = jnp.concatenate([t, t], axis=1)
        return jnp.concatenate([t, jnp.full((tm, LANES), fill, F32)], axis=0)

    return full(cos64, 1.0), full(slo64, 0.0), full(shi64, 0.0)


def _pad_cols(w, n):
    return jnp.pad(w, ((0, 0), (0, n - w.shape[1])))


def kernel(x, c, ctx, c_ctx, ada_w, ada_b, norm_g, ffn_w_in, ffn_w_out, even_w_in, even_w_out, diff_lambda,
           diff_norm_g, mlstm_conv_w, mlstm_conv_b, mlstm_gate_b, mlstm_norm_g, odd_w_in, odd_w_out,
           gla_w_gate, gla_b_gate, gla_norm_g, final_g):
    bsz, seq, d = x.shape
    n_ctx = ctx.shape[1]
    depth = ada_w.shape[0]
    r_lat, r_ctx = bsz * seq, bsz * n_ctx
    tm = min(ROW_TILE, seq, r_ctx)
    assert seq % tm == 0 and r_ctx % tm == 0 and tm % ATT_VBLK == 0 and seq % ATT_TK == 0
    assert seq % ML_CHUNK == 0 and n_ctx % ML_CHUNK == 0 and n_ctx % ATT_TQ == 0 and r_lat % n_ctx == 0
    assert seq & (seq - 1) == 0 and n_ctx & (n_ctx - 1) == 0 and n_ctx % GLA_BLOCK == 0
    n_lat_tiles, per_batch = r_lat // tm, seq // tm
    lay = {
        "b": bsz, "seq": seq, "ctx": n_ctx, "tm": tm,
        "mod_index": lambda i: jnp.where(i < n_lat_tiles, 1 + i // per_batch, 0),
        "rope_index": lambda i: jnp.where(i < n_lat_tiles, i % per_batch, per_batch),
    }
    n_all = (r_lat + r_ctx) // tm

    n_s = -(-(bsz + 1) // 8) * 8
    svec = jnp.zeros((n_s, d), F32).at[:bsz].set(c).at[bsz].set(c_ctx)
    mods = _modulation(svec, ada_w, ada_b)
    mods = jnp.concatenate([mods[:, bsz:bsz + 1], mods[:, :bsz]], axis=1).reshape(depth, bsz + 1, N_MOD, d)

    rope = _rope_tables(seq, tm)
    xs, tail = x.reshape(r_lat, d), ctx.reshape(r_ctx, d)
    w_in_b = ffn_w_in.astype(BF16)
    w_out_b = ffn_w_out.astype(BF16)

    for l in range(depth):
        last = l == depth - 1
        mod = mods[l]
        xs = _ffn(xs, mod, norm_g[l, 0], w_in_b[l, 0], w_out_b[l, 0], final_g, lay=lay, n_tiles=n_all, base=0, final=False,
                  tail=tail if l == 0 else None)
        n_keep = n_lat_tiles if last else n_all
        if l % 2 == 0:
            e = l // 2
            lam_init = 0.8 - 0.6 * math.exp(-0.3 * l)
            w = _pad_cols(even_w_in[e], 3200).astype(BF16)
            gate_b = _pad_cols(mlstm_gate_b[e].reshape(1, -1), LANES)
            q, k, vt, mlqk, mlv, opre, gates = _even_proj(xs, mod, norm_g[l, 1], w, gate_b, rope, lay=lay)
            a = _attention(q, k, vt, diff_lambda[e], diff_norm_g[e], lay=lay, lam_init=lam_init)
            qm, km = _ml_conv(mlqk, mlstm_conv_w[e], mlstm_conv_b[e], lay=lay)
            mf, mb = _mlstm(qm, km, mlv, gates, lay=lay)
            xs = _even_merge(xs, mod, a, mf, mb, opre, mlstm_norm_g[e], even_w_out[e].astype(BF16), lay=lay, n_tiles=n_keep)
        else:
            o = l // 2
            w = _pad_cols(odd_w_in[o], 3200).astype(BF16)
            wg = jnp.zeros((LANES, 1024), F32)
            wg = wg.at[0:GLA_RANK, 0:512].set(gla_w_gate[o, 0]).at[GLA_RANK:2 * GLA_RANK, 512:1024].set(gla_w_gate[o, 1])
            bg = gla_b_gate[o].reshape(1, 1024)
            q, k, v, rgate, la_f, la_b = _odd_proj(xs, mod, norm_g[l, 1], w, wg.astype(BF16), bg, lay=lay)
            of, ob = _gla(q, k, v, la_f, la_b, lay=lay)
            xs = _odd_merge(xs, mod, of, ob, rgate, gla_norm_g[o], odd_w_out[o].astype(BF16), lay=lay, n_tiles=n_keep)
        xs = _ffn(xs, mod, norm_g[l, 2], w_in_b[l, 1], w_out_b[l, 1], final_g, lay=lay, n_tiles=n_keep, base=6, final=last)
    return xs[:r_lat].reshape(bsz, seq, d)
```

```python
import functools
import math

import jax
import jax.numpy as jnp
import numpy as np
from jax import lax
from jax.experimental import pallas as pl
from jax.experimental.pallas import tpu as pltpu

F32 = jnp.float32
BF16 = jnp.bfloat16

EPS = 1e-6
N_MOD = 9
GRID_W = 64
ROPE_BASE = 10000.0
DA_HEADS, DA_HD, DA_VD = 4, 64, 128
ML_HEADS, ML_QK, ML_V = 4, 64, 128
GLA_HEADS, GLA_K, GLA_V, GLA_RANK, GLA_TAU = 4, 128, 256, 16, 16.0

LANES = 128
MXU_DIM = 256
VMEM_LIMIT = 56 * 1024 * 1024
NEG = -1e30
LOG2E = math.log2(math.e)

ROW_TILE = 1024
FF_TILE = 256
MIX_TILE = 512
ATT_TQ = 256
ATT_TK = 512
ATT_VBLK = 256
ML_CHUNK = 256
GLA_CHUNK = 128
GLA_BLOCK = 256
CONV_TILE = 1024


def _mod_index(lay, tm):
    n_lat, per_batch = lay["b"] * lay["seq"] // tm, lay["seq"] // tm
    return lambda i: jnp.where(i < n_lat, 1 + i // per_batch, 0)


def _cparams(sem):
    return pltpu.CompilerParams(dimension_semantics=sem, vmem_limit_bytes=VMEM_LIMIT)


def _sigmoid(x):
    return 1.0 / (1.0 + jnp.exp(-x))


def _silu(x):
    return x * _sigmoid(x)


def _log_sigmoid(x):
    return jnp.minimum(x, 0.0) - jnp.log1p(jnp.exp(-jnp.abs(x)))


def _rms(x, axis=-1):
    return x * lax.rsqrt(jnp.mean(x * x, axis=axis, keepdims=True) + EPS)


def _ada_norm(x, g, shift, scale):
    return _rms(x) * g * (1.0 + scale) + shift


def _dot(a, b):
    return jnp.dot(a, b, preferred_element_type=F32)


def _dot_nt(a, b):
    return lax.dot_general(a, b, (((1,), (1,)), ((), ())), preferred_element_type=F32)


def _dot_tn(a, b):
    return lax.dot_general(a, b, (((0,), (0,)), ((), ())), preferred_element_type=F32)


def _scan_lanes(x, reverse, op, fill):
    n = x.shape[1]
    lane = lax.broadcasted_iota(jnp.int32, x.shape, 1)
    k = 1
    while k < n:
        if reverse:
            x = op(x, jnp.where(lane < n - k, pltpu.roll(x, n - k, 1), fill))
        else:
            x = op(x, jnp.where(lane >= k, pltpu.roll(x, k, 1), fill))
        k *= 2
    return x


def _tri_cumsum(tri, x):
    if x.dtype == BF16:
        return _dot(tri, x)
    hi = x.astype(BF16)
    lo = (x - hi.astype(F32)).astype(BF16)
    r = _dot(tri, jnp.concatenate([hi, lo], axis=1))
    w = x.shape[1]
    return r[:, 0:w] + r[:, w:2 * w]


def _mod_kernel(s_ref, w_ref, b_ref, o_ref):
    s = _silu(s_ref[...])
    o_ref[0] = jnp.dot(s, w_ref[0], preferred_element_type=F32,
                       precision=lax.Precision.HIGHEST) + b_ref[0]


def _modulation(svec, ada_w, ada_b):
    depth, d, n = ada_w.shape
    tn = d
    return pl.pallas_call(
        _mod_kernel,
        grid=(depth, n // tn),
        in_specs=[pl.BlockSpec(svec.shape, lambda l, j: (0, 0)),
                  pl.BlockSpec((1, d, tn), lambda l, j: (l, 0, j)),
                  pl.BlockSpec((1, 1, tn), lambda l, j: (l, 0, j))],
        out_specs=pl.BlockSpec((1, svec.shape[0], tn), lambda l, j: (l, 0, j)),
        out_shape=jax.ShapeDtypeStruct((depth, svec.shape[0], n), F32),
        compiler_params=_cparams(("parallel", "parallel")),
    )(svec, ada_w, ada_b.reshape(depth, 1, n))


def _group_rms(x, width):
    outs = []
    for h in range(x.shape[1] // width):
        outs.append(_rms(x[:, h * width:(h + 1) * width]))
    return jnp.concatenate(outs, axis=1)


def _ffn_kernel(*refs, base, final, n_first, mix):
    x_ref, c_ref, mod_ref, g_ref, wi_ref, wo_ref, fg_ref = refs[:7]
    o_ref, u_ref = refs[-2:]
    dff = wo_ref.shape[0]
    x = x_ref[...]
    if n_first is not None:
        x = jnp.where(pl.program_id(0) < n_first, x, c_ref[...])
    if mix == "even":
        a_ref, mf_ref, mb_ref, op_ref, gm_ref, wm_ref = refs[7:13]
        mem = mf_ref[...].astype(F32) + mb_ref[...].astype(F32)
        m = _group_rms(mem, ML_V) * gm_ref[...] * _sigmoid(op_ref[...].astype(F32))
        y = _dot(a_ref[...], wm_ref[0:512, :]) + _dot(m.astype(BF16), wm_ref[512:1024, :])
        x = x + mod_ref[0, 5:6, :] * y
    elif mix == "odd":
        of_ref, ob_ref, r_ref, gm_ref, wm_ref = refs[7:12]
        o = of_ref[...].astype(F32) + ob_ref[...].astype(F32)
        y = _group_rms(o, GLA_V) * gm_ref[...] * _silu(r_ref[...].astype(F32))
        x = x + mod_ref[0, 5:6, :] * _dot(y.astype(BF16), wm_ref[...])
    h = _ada_norm(x, g_ref[...], mod_ref[0, base:base + 1, :], mod_ref[0, base + 1:base + 2, :]).astype(BF16)
    for j in range(dff // FF_TILE):
        a = _dot(h, wi_ref[:, j * FF_TILE:(j + 1) * FF_TILE])
        b = _dot(h, wi_ref[:, dff + j * FF_TILE:dff + (j + 1) * FF_TILE])
        u_ref[:, j * FF_TILE:(j + 1) * FF_TILE] = (_silu(a) * b).astype(BF16)
    y = x + 0.5 * mod_ref[0, base + 2:base + 3, :] * _dot(u_ref[...], wo_ref[...])
    if final:
        y = _rms(y) * fg_ref[...]
    o_ref[...] = y


def _ffn(xs, mod, g, w_in, w_out, widx, final_g, *, lay, tm, n_rows, base, final, tail=None, mix=None, mix_args=()):
    d = xs.shape[1]
    dff = w_out.shape[2]
    n_tiles = n_rows // tm
    wmap = lambda i: widx + (0, 0)
    midx = _mod_index(lay, tm)
    const = lambda i: (0, 0)
    row = lambda i: (i, 0)
    once = pl.Buffered(1)
    if tail is None:
        n_first, tail, xmap, cspec = None, xs, row, pl.BlockSpec((8, d), const)
    else:
        n_first = xs.shape[0] // tm
        xmap = lambda i: (jnp.minimum(i, n_first - 1), 0)
        cspec = pl.BlockSpec((tm, d), lambda i: (jnp.maximum(i - n_first, 0), 0))
    mix_specs = []
    for arr in mix_args:
        if arr.shape[0] == 1:
            mix_specs.append(pl.BlockSpec(arr.shape, const))
        elif arr.shape[0] == d and arr.shape[1] == d:
            mix_specs.append(pl.BlockSpec(arr.shape, const, pipeline_mode=once))
        else:
            mix_specs.append(pl.BlockSpec((tm, arr.shape[1]), row))
    return pl.pallas_call(
        functools.partial(_ffn_kernel, base=base, final=final, n_first=n_first, mix=mix),
        grid=(n_tiles,),
        in_specs=[pl.BlockSpec((tm, d), xmap),
                  cspec,
                  pl.BlockSpec((1, N_MOD, d), lambda i: (midx(i), 0, 0)),
                  pl.BlockSpec((1, d), const),
                  pl.BlockSpec((None, None) + w_in.shape[2:], wmap, pipeline_mode=once),
                  pl.BlockSpec((None, None) + w_out.shape[2:], wmap, pipeline_mode=once),
                  pl.BlockSpec((1, d), const)] + mix_specs,
        out_specs=pl.BlockSpec((tm, d), row),
        out_shape=jax.ShapeDtypeStruct((n_rows, d), F32),
        scratch_shapes=[pltpu.VMEM((tm, dff), BF16)],
        compiler_params=_cparams(("parallel",)),
    )(xs, tail, mod, g.reshape(1, d), w_in, w_out, final_g.reshape(1, d), *mix_args)


def _rope_heads(r, c, slo, shi, scale):
    outs = []
    for h in range(DA_HEADS):
        rh = r[:, h * LANES:(h + 1) * LANES]
        y = rh * c + pltpu.roll(rh, LANES - 16, 1) * slo + pltpu.roll(rh, 16, 1) * shi
        outs.append(y * scale if scale != 1.0 else y)
    return jnp.concatenate(outs, axis=1)


def _even_proj_kernel(x_ref, mod_ref, g_ref, w_ref, gb_ref, c_ref, slo_ref, shi_ref,
                      q_ref, k_ref, vt_ref, mlqk_ref, mlv_ref, opre_ref, gate_ref):
    h = _ada_norm(x_ref[...], g_ref[...], mod_ref[0, 3:4, :], mod_ref[0, 4:5, :]).astype(BF16)
    c, slo, shi = c_ref[...], slo_ref[...], shi_ref[...]
    q_ref[...] = _rope_heads(_dot(h, w_ref[:, 0:512]), c, slo, shi, DA_HD ** -0.5 * LOG2E).astype(BF16)
    k_ref[...] = _rope_heads(_dot(h, w_ref[:, 512:1024]), c, slo, shi, 1.0).astype(BF16)
    vt = _dot(h, w_ref[:, 1024:1536]).T.astype(BF16)
    for i in range(vt_ref.shape[0]):
        vt_ref[i] = vt[:, i * ATT_VBLK:(i + 1) * ATT_VBLK]
    mlqk_ref[...] = _dot(h, w_ref[:, 1536:2048])
    mlv_ref[...] = _dot(h, w_ref[:, 2048:2560]).astype(BF16)
    opre_ref[...] = _dot(h, w_ref[:, 2560:3072]).astype(BF16)
    gate_ref[...] = _dot(h, w_ref[:, 3072:3200]) + gb_ref[...]


def _even_proj(xs, mod, g, w, gate_b, rope, *, lay):
    r, d = xs.shape
    tm = lay["tm"]
    n_tiles = r // tm
    midx, ridx = lay["mod_index"], lay["rope_index"]
    row = lambda i: (i, 0)
    const = lambda i: (0, 0)
    tab = pl.BlockSpec((tm, LANES), lambda i: (ridx(i), 0))
    return pl.pallas_call(
        _even_proj_kernel,
        grid=(n_tiles,),
        in_specs=[pl.BlockSpec((tm, d), row),
                  pl.BlockSpec((1, N_MOD, d), lambda i: (midx(i), 0, 0)),
                  pl.BlockSpec((1, d), const),
                  pl.BlockSpec(w.shape, const),
                  pl.BlockSpec((1, LANES), const),
                  tab, tab, tab],
        out_specs=[pl.BlockSpec((tm, 512), row), pl.BlockSpec((tm, 512), row),
                   pl.BlockSpec((tm // ATT_VBLK, 512, ATT_VBLK), lambda i: (i, 0, 0)),
                   pl.BlockSpec((tm, 512), row), pl.BlockSpec((tm, 512), row),
                   pl.BlockSpec((tm, 512), row), pl.BlockSpec((tm, LANES), row)],
        out_shape=[jax.ShapeDtypeStruct((r, 512), BF16), jax.ShapeDtypeStruct((r, 512), BF16),
                   jax.ShapeDtypeStruct((r // ATT_VBLK, 512, ATT_VBLK), BF16),
                   jax.ShapeDtypeStruct((r, 512), F32), jax.ShapeDtypeStruct((r, 512), BF16),
                   jax.ShapeDtypeStruct((r, 512), BF16), jax.ShapeDtypeStruct((r, LANES), F32)],
        compiler_params=_cparams(("parallel",)),
    )(xs, mod, g.reshape(1, d), w, gate_b, *rope)


def _attn_kernel(lam_ref, g_ref, q_ref, kl_ref, kc_ref, vl_ref, vc_ref, o_ref, acc_ref, m_ref, l_ref,
                 *, nql, n_lat_chunks, n_ctx_blocks, lam_init):
    t = pl.program_id(1)
    tq, vb = ATT_TQ, ATT_VBLK
    per = ATT_TK // vb
    m_ref[...] = jnp.full(m_ref.shape, NEG, F32)
    l_ref[...] = jnp.zeros_like(l_ref)
    acc_ref[...] = jnp.zeros_like(acc_ref)

    q = q_ref[...]
    lane = lax.broadcasted_iota(jnp.int32, (tq, LANES), 1)
    qcat = []
    for h in range(DA_HEADS):
        qh = q[:, h * LANES:(h + 1) * LANES]
        qcat.append(jnp.concatenate([jnp.where(lane < DA_HD, qh, jnp.zeros_like(qh)),
                                     jnp.where(lane >= DA_HD, qh, jnp.zeros_like(qh))], axis=0))

    def block(k_of, v_of):
        ss = [_dot_nt(k_of(h), qcat[h]) for h in range(DA_HEADS)]
        alphas, ps = [], []
        for h in range(DA_HEADS):
            m_old = m_ref[h]
            m_new = jnp.maximum(m_old, jnp.max(ss[h], axis=0, keepdims=True))
            alpha = jnp.exp2(m_old - m_new)
            p = jnp.exp2(ss[h] - m_new)
            l_ref[h] = alpha * l_ref[h] + jnp.sum(p, axis=0, keepdims=True)
            m_ref[h] = m_new
            alphas.append(alpha)
            ps.append(p.astype(BF16))
        for h in range(DA_HEADS):
            acc_ref[h] = alphas[h] * acc_ref[h] + _dot(v_of(h), ps[h])

    for j in range(n_ctx_blocks):
        block(lambda h, j=j: kc_ref[j * vb:(j + 1) * vb, h * LANES:(h + 1) * LANES],
              lambda h, j=j: vc_ref[j, h * LANES:(h + 1) * LANES, :])

    @pl.when(t < nql)
    def _():
        def body(j, carry):
            r0 = pl.multiple_of(j * ATT_TK, ATT_TK)
            block(lambda h: kl_ref[pl.ds(r0, ATT_TK), h * LANES:(h + 1) * LANES],
                  lambda h: jnp.concatenate([vl_ref[j * per + i, h * LANES:(h + 1) * LANES, :] for i in range(per)], axis=1))
            return carry
        lax.fori_loop(0, n_lat_chunks, body, 0)

    lp = lam_ref[...]
    lam = (jnp.exp(jnp.sum(lp[0:1] * lp[1:2], axis=1, keepdims=True))
           - jnp.exp(jnp.sum(lp[2:3] * lp[3:4], axis=1, keepdims=True)) + lam_init)
    g = g_ref[...] * (1.0 - lam_init)
    for h in range(DA_HEADS):
        on = acc_ref[h] / l_ref[h]
        y = _rms(on[:, 0:tq] - lam * on[:, tq:2 * tq], axis=0) * g
        o_ref[:, h * LANES:(h + 1) * LANES] = y.T.astype(BF16)


def _attention(q, k, vt, diff_lambda, diff_norm_g, *, lay, lam_init):
    bsz, seq, ctx = lay["b"], lay["seq"], lay["ctx"]
    r_lat = bsz * seq
    tq, vb = ATT_TQ, ATT_VBLK
    nql, nqc, nkl, nkc = seq // tq, ctx // tq, seq // vb, ctx // vb
    nlq = r_lat // tq
    nctx0 = r_lat // ctx

    def qidx(b, t):
        return (jnp.where(t < nql, b * nql + t, nlq + b * nqc + (t - nql)), 0)

    return pl.pallas_call(
        functools.partial(_attn_kernel, nql=nql, n_lat_chunks=seq // ATT_TK, n_ctx_blocks=nkc, lam_init=lam_init),
        grid=(bsz, nql + nqc),
        in_specs=[pl.BlockSpec((4, DA_HD), lambda b, t: (0, 0)),
                  pl.BlockSpec((DA_VD, 1), lambda b, t: (0, 0)),
                  pl.BlockSpec((tq, 512), qidx),
                  pl.BlockSpec((seq, 512), lambda b, t: (b, 0)),
                  pl.BlockSpec((ctx, 512), lambda b, t: (nctx0 + b, 0)),
                  pl.BlockSpec((nkl, 512, vb), lambda b, t: (b, 0, 0)),
                  pl.BlockSpec((nkc, 512, vb), lambda b, t: (nctx0 + b, 0, 0))],
        out_specs=pl.BlockSpec((tq, 512), qidx),
        out_shape=jax.ShapeDtypeStruct(q.shape, BF16),
        scratch_shapes=[pltpu.VMEM((DA_HEADS, DA_VD, 2 * tq), F32), pltpu.VMEM((DA_HEADS, 1, 2 * tq), F32),
                        pltpu.VMEM((DA_HEADS, 1, 2 * tq), F32)],
        compiler_params=_cparams(("parallel", "parallel")),
    )(diff_lambda, diff_norm_g.reshape(DA_VD, 1), q, k, k, vt, vt)


def _conv_kernel(x_ref, prev_ref, next_ref, w_ref, b_ref, q_ref, k_ref, *, lay):
    i = pl.program_id(0)
    tc = x_ref.shape[0]
    seq_len = jnp.where(i < lay["b"] * lay["seq"] // tc, lay["seq"], lay["ctx"])
    x = x_ref[...]
    rows = lax.broadcasted_iota(jnp.int32, x.shape, 0)
    pos = (i * tc + rows) & (seq_len - 1)
    up = jnp.where(rows == 0, prev_ref[7:8, :], pltpu.roll(x, 1, 0))
    dn = jnp.where(rows == tc - 1, next_ref[0:1, :], pltpu.roll(x, tc - 1, 0))
    up = jnp.where(pos == 0, 0.0, up)
    dn = jnp.where(pos == seq_len - 1, 0.0, dn)
    y = _silu(w_ref[0:1, :] * up + w_ref[1:2, :] * x + w_ref[2:3, :] * dn + b_ref[...])
    q_ref[...] = (y[:, 0:256] * (ML_QK ** -0.5)).astype(BF16)
    k_ref[...] = y[:, 256:512].astype(BF16)


def _ml_conv(mlqk, conv_w, conv_b, *, lay):
    r = mlqk.shape[0]
    tc = min(CONV_TILE, lay["tm"])
    n = r // tc
    sub = tc // 8
    last = r // 8 - 1
    return pl.pallas_call(
        functools.partial(_conv_kernel, lay=lay),
        grid=(n,),
        in_specs=[pl.BlockSpec((tc, 512), lambda i: (i, 0)),
                  pl.BlockSpec((8, 512), lambda i: (jnp.maximum(i * sub - 1, 0), 0)),
                  pl.BlockSpec((8, 512), lambda i: (jnp.minimum((i + 1) * sub, last), 0)),
                  pl.BlockSpec((3, 512), lambda i: (0, 0)),
                  pl.BlockSpec((1, 512), lambda i: (0, 0))],
        out_specs=[pl.BlockSpec((tc, 256), lambda i: (i, 0)), pl.BlockSpec((tc, 256), lambda i: (i, 0))],
        out_shape=[jax.ShapeDtypeStruct((r, 256), BF16), jax.ShapeDtypeStruct((r, 256), BF16)],
        compiler_params=_cparams(("parallel",)),
    )(mlqk, mlqk, mlqk, conv_w, conv_b.reshape(1, 512))


def _scan_indices(lay, chunk):
    bsz, seq, ctx = lay["b"], lay["seq"], lay["ctx"]
    nc_lat, nc_ctx = seq // chunk, ctx // chunk
    nl = bsz * nc_lat

    def fwd(b, j):
        return jnp.where(j < nc_ctx, nl + b * nc_ctx + j, b * nc_lat + (j - nc_ctx))

    def bwd(b, j):
        return jnp.where(j < nc_ctx, nl + b * nc_ctx + (nc_ctx - 1 - j), b * nc_lat + (nc_lat - 1 - (j - nc_ctx)))

    return fwd, bwd, nc_ctx + nc_lat


def _mlstm_step(dirs, ctn_ref, m_ref):
    n = dirs[0][0].shape[0]
    rows = lax.broadcasted_iota(jnp.int32, (n, n), 0)
    cols = lax.broadcasted_iota(jnp.int32, (n, n), 1)
    lane_q = lax.broadcasted_iota(jnp.int32, (n, ML_HEADS * ML_QK), 1)
    heads = [(lane_q >= h * ML_QK) & (lane_q < (h + 1) * ML_QK) for h in range(ML_HEADS)]
    pairs = [(d, h) for d in range(2) for h in range(ML_HEADS)]
    ones = jnp.ones((n, LANES), BF16)

    pre = []
    for d, (q, k, v, gates, _) in enumerate(dirs):
        gt = gates.T
        edge = 0 if d == 1 else n - 1
        bc = _scan_lanes(_log_sigmoid(gt[8:16, :]), d == 1, jnp.add, 0.0)
        cp = gt[0:8, :] - bc
        m_old = m_ref[d][:, 0:1]
        mx = jnp.maximum(m_old, _scan_lanes(cp, d == 1, jnp.maximum, NEG))
        mx_end = mx[:, edge:edge + 1]
        m_ref[d] = jnp.broadcast_to(bc[:, edge:edge + 1] + mx_end, (8, LANES))
        stack = jnp.concatenate([mx, bc + mx, jnp.exp(cp - mx_end), jnp.zeros((LANES - 24, n), F32)], axis=0)
        pre.append((cp, m_old, jnp.exp(m_old - mx_end), stack.T, ctn_ref[d]))

    qhs, s_raw, r2 = {}, {}, {}
    for d, h in pairs:
        q = dirs[d][0]
        qhs[d, h] = jnp.where(heads[h], q, jnp.zeros_like(q))
    for d, h in pairs:
        s_raw[d, h] = _dot_nt(qhs[d, h], dirs[d][1])
    for d, h in pairs:
        r2[d, h] = _dot(qhs[d, h], pre[d][4].astype(BF16))

    p = {}
    for d, h in pairs:
        cp, _, _, colf, _ = pre[d]
        r = d * ML_HEADS + h
        mask = (cols >= rows) if d == 1 else (cols <= rows)
        mxb = jnp.broadcast_to(colf[:, r:r + 1], (n, LANES))
        dm = jnp.where(mask, cp[r:r + 1, :] - jnp.concatenate([mxb] * (n // LANES), axis=1), NEG)
        p[d, h] = (s_raw[d, h] * jnp.exp(dm)).astype(BF16)

    nd = {}
    for d, h in pairs:
        v = dirs[d][2]
        nd[d, h] = _dot(p[d, h], jnp.concatenate([v[:, h * ML_V:(h + 1) * ML_V], ones], axis=1))

    for d, h in pairs:
        _, m_old, _, colf, _ = pre[d]
        r = d * ML_HEADS + h
        w = jnp.exp(m_old[r:r + 1, :] - jnp.broadcast_to(colf[:, r:r + 1], (n, LANES)))
        floor = jnp.exp(-jnp.broadcast_to(colf[:, 8 + r:9 + r], (n, LANES)))
        tot = nd[d, h] + jnp.concatenate([w, w], axis=1) * r2[d, h]
        hid = tot[:, 0:ML_V] / jnp.maximum(jnp.abs(tot[:, ML_V:2 * ML_V]), floor)
        dirs[d][4][:, h * ML_V:(h + 1) * ML_V] = hid.astype(BF16)

    for d, (q, k, v, gates, _) in enumerate(dirs):
        _, _, decay, colf, ctn = pre[d]
        wk = jnp.zeros((n, ML_HEADS * ML_QK), F32)
        for h in range(ML_HEADS):
            r = d * ML_HEADS + h
            wk = jnp.where(heads[h], colf[:, 16 + r:17 + r], wk)
        kw = (k.astype(F32) * wk).astype(BF16)
        u = _dot_tn(kw, jnp.concatenate([v, ones], axis=1))
        for h in range(ML_HEADS):
            r0, r = h * ML_QK, d * ML_HEADS + h
            dec = decay[r:r + 1, :]
            ctn_ref[d, r0:r0 + ML_QK, 0:ML_V] = dec * ctn[r0:r0 + ML_QK, 0:ML_V] + u[r0:r0 + ML_QK, h * ML_V:(h + 1) * ML_V]
            ctn_ref[d, r0:r0 + ML_QK, ML_V:2 * ML_V] = (dec * ctn[r0:r0 + ML_QK, ML_V:2 * ML_V]
                                                        + u[r0:r0 + ML_QK, 4 * ML_V:5 * ML_V])


def _mlstm_kernel(qf_ref, kf_ref, vf_ref, gf_ref, qb_ref, kb_ref, vb_ref, gb_ref, of_ref, ob_ref, ctn_ref, m_ref):
    @pl.when(pl.program_id(1) == 0)
    def _():
        ctn_ref[...] = jnp.zeros_like(ctn_ref)
        m_ref[...] = jnp.zeros_like(m_ref)

    _mlstm_step([(qf_ref[...], kf_ref[...], vf_ref[...], gf_ref[...], of_ref),
                 (qb_ref[...], kb_ref[...], vb_ref[...], gb_ref[...], ob_ref)], ctn_ref, m_ref)


def _mlstm(q, k, v, gates, *, lay):
    n = ML_CHUNK
    fwd, bwd, steps = _scan_indices(lay, n)
    r = q.shape[0]

    def specs(idx):
        return [pl.BlockSpec((n, 256), lambda b, j: (idx(b, j), 0)),
                pl.BlockSpec((n, 256), lambda b, j: (idx(b, j), 0)),
                pl.BlockSpec((n, 512), lambda b, j: (idx(b, j), 0)),
                pl.BlockSpec((n, LANES), lambda b, j: (idx(b, j), 0))]

    return pl.pallas_call(
        _mlstm_kernel,
        grid=(lay["b"], steps),
        in_specs=specs(fwd) + specs(bwd),
        out_specs=[pl.BlockSpec((n, 512), lambda b, j: (fwd(b, j), 0)),
                   pl.BlockSpec((n, 512), lambda b, j: (bwd(b, j), 0))],
        out_shape=[jax.ShapeDtypeStruct((r, 512), BF16), jax.ShapeDtypeStruct((r, 512), BF16)],
        scratch_shapes=[pltpu.VMEM((2, 256, 256), F32), pltpu.VMEM((2, 8, LANES), F32)],
        compiler_params=_cparams(("parallel", "arbitrary")),
    )(q, k, v, gates, q, k, v, gates)


def _gla_step(dirs, st_ref):
    n = dirs[0][0].shape[0]
    rows = lax.broadcasted_iota(jnp.int32, (n, n), 0)
    cols = lax.broadcasted_iota(jnp.int32, (n, n), 1)
    mid = n // 2
    pairs = [(d, h) for d in range(2) for h in range(GLA_HEADS)]
    ks = [slice(h * GLA_K, (h + 1) * GLA_K) for h in range(GLA_HEADS)]
    vs = [slice(h * GLA_V, (h + 1) * GLA_V) for h in range(GLA_HEADS)]

    masks = [cols <= rows, cols >= rows]
    bcs = [_tri_cumsum(jnp.where(masks[d], 1.0, 0.0).astype(BF16), dirs[d][3]) for d in range(2)]

    qe, ke, qs, kd, dec, sts = {}, {}, {}, {}, {}, {}
    for d, h in pairs:
        q, k, _, la, _ = dirs[d]
        edge = 0 if d == 1 else n - 1
        bc = bcs[d][:, ks[h]]
        bmid, bl = bc[mid:mid + 1, :], bc[edge:edge + 1, :]
        e_up, e_dn = jnp.exp(bc - bmid), jnp.exp(bmid - bc)
        qf, kf = q[:, ks[h]].astype(F32), k[:, ks[h]].astype(F32)
        qe[d, h] = (qf * e_up).astype(BF16)
        ke[d, h] = (kf * e_dn).astype(BF16)
        qs[d, h] = (qf * (e_up * jnp.exp(bmid))).astype(BF16)
        kd[d, h] = (kf * (e_dn * jnp.exp(bl - bmid))).astype(BF16)
        dec[d, h] = jnp.exp(bl)
        sts[d, h] = st_ref[d, h]

    a, inter = {}, {}
    for d, h in pairs:
        a[d, h] = _dot_nt(qe[d, h], ke[d, h])
    for d, h in pairs:
        inter[d, h] = _dot_nt(qs[d, h], sts[d, h].astype(BF16))
    for d, h in pairs:
        a[d, h] = jnp.where(masks[d], a[d, h], 0.0).astype(BF16)
    for d, h in pairs:
        dirs[d][4][:, vs[h]] = (_dot(a[d, h], dirs[d][2][:, vs[h]]) + inter[d, h]).astype(BF16)
    for d, h in pairs:
        st_ref[d, h] = sts[d, h] * dec[d, h] + _dot_tn(dirs[d][2][:, vs[h]], kd[d, h])


def _gla_kernel(qf_ref, kf_ref, vf_ref, af_ref, qb_ref, kb_ref, vb_ref, ab_ref, of_ref, ob_ref, st_ref):
    @pl.when(pl.program_id(1) == 0)
    def _():
        st_ref[...] = jnp.zeros_like(st_ref)

    n = GLA_CHUNK
    per = qf_ref.shape[0] // n
    for c in range(per):
        f, b = pl.ds(c * n, n), pl.ds((per - 1 - c) * n, n)
        _gla_step([(qf_ref[f, :], kf_ref[f, :], vf_ref[f, :], af_ref[f, :], of_ref.at[f]),
                   (qb_ref[b, :], kb_ref[b, :], vb_ref[b, :], ab_ref[b, :], ob_ref.at[b])], st_ref)


def _gla(q, k, v, la_f, la_b, *, lay):
    n = GLA_BLOCK
    fwd, bwd, steps = _scan_indices(lay, n)
    r = q.shape[0]

    def specs(idx):
        return [pl.BlockSpec((n, 512), lambda b, j: (idx(b, j), 0)),
                pl.BlockSpec((n, 512), lambda b, j: (idx(b, j), 0)),
                pl.BlockSpec((n, 1024), lambda b, j: (idx(b, j), 0)),
                pl.BlockSpec((n, 512), lambda b, j: (idx(b, j), 0))]

    return pl.pallas_call(
        _gla_kernel,
        grid=(lay["b"], steps),
        in_specs=specs(fwd) + specs(bwd),
        out_specs=[pl.BlockSpec((n, 1024), lambda b, j: (fwd(b, j), 0)),
                   pl.BlockSpec((n, 1024), lambda b, j: (bwd(b, j), 0))],
        out_shape=[jax.ShapeDtypeStruct((r, 1024), BF16), jax.ShapeDtypeStruct((r, 1024), BF16)],
        scratch_shapes=[pltpu.VMEM((2, GLA_HEADS, GLA_V, GLA_K), F32)],
        compiler_params=_cparams(("parallel", "arbitrary")),
    )(q, k, v, la_f, q, k, v, la_b)


def _odd_proj_kernel(x_ref, mod_ref, g_ref, w_ref, wg_ref, bg_ref, q_ref, k_ref, v_ref, r_ref, laf_ref, lab_ref):
    h = _ada_norm(x_ref[...], g_ref[...], mod_ref[0, 3:4, :], mod_ref[0, 4:5, :]).astype(BF16)
    q_ref[...] = (_dot(h, w_ref[:, 0:512]) * (GLA_K ** -0.5)).astype(BF16)
    k_ref[...] = _dot(h, w_ref[:, 512:1024]).astype(BF16)
    v_ref[...] = _dot(h, w_ref[:, 1024:2048]).astype(BF16)
    r_ref[...] = _dot(h, w_ref[:, 2048:3072]).astype(BF16)
    lr = _dot(h, w_ref[:, 3072:3200]).astype(BF16)
    la = _log_sigmoid(_dot(lr, wg_ref[...]) + bg_ref[...]) * (1.0 / GLA_TAU)
    laf_ref[...] = la[:, 0:512].astype(BF16)
    lab_ref[...] = la[:, 512:1024].astype(BF16)


def _odd_proj(xs, mod, g, w, wg, bg, *, lay):
    r, d = xs.shape
    tm = lay["tm"]
    midx = lay["mod_index"]
    row = lambda i: (i, 0)
    const = lambda i: (0, 0)
    return pl.pallas_call(
        _odd_proj_kernel,
        grid=(r // tm,),
        in_specs=[pl.BlockSpec((tm, d), row),
                  pl.BlockSpec((1, N_MOD, d), lambda i: (midx(i), 0, 0)),
                  pl.BlockSpec((1, d), const), pl.BlockSpec(w.shape, const),
                  pl.BlockSpec(wg.shape, const), pl.BlockSpec((1, 1024), const)],
        out_specs=[pl.BlockSpec((tm, 512), row), pl.BlockSpec((tm, 512), row), pl.BlockSpec((tm, 1024), row),
                   pl.BlockSpec((tm, 1024), row), pl.BlockSpec((tm, 512), row), pl.BlockSpec((tm, 512), row)],
        out_shape=[jax.ShapeDtypeStruct((r, 512), BF16), jax.ShapeDtypeStruct((r, 512), BF16),
                   jax.ShapeDtypeStruct((r, 1024), BF16), jax.ShapeDtypeStruct((r, 1024), BF16),
                   jax.ShapeDtypeStruct((r, 512), BF16), jax.ShapeDtypeStruct((r, 512), BF16)],
        compiler_params=_cparams(("parallel",)),
    )(xs, mod, g.reshape(1, d), w, wg, bg)


def _rope_tables(seq, tm):
    half = DA_HD // 4
    f32 = np.float32
    row = np.repeat(np.arange(seq // GRID_W), GRID_W).astype(f32)
    col = (np.arange(seq) % GRID_W).astype(f32)
    inv = (f32(ROPE_BASE) ** (-np.arange(half, dtype=f32) * f32(2.0) / f32(2 * half))).astype(f32)
    ar, ac = row[:, None] * inv, col[:, None] * inv
    cos64 = np.concatenate([np.cos(ar), np.cos(ar), np.cos(ac), np.cos(ac)], axis=1)
    sin_r, sin_c, z = np.sin(ar), np.sin(ac), np.zeros_like(ar)
    slo64 = np.concatenate([-sin_r, z, -sin_c, z], axis=1)
    shi64 = np.concatenate([z, sin_r, z, sin_c], axis=1)

    def full(t, fill):
        t = np.concatenate([t, t], axis=1)
        return np.concatenate([t, np.full((tm, LANES), fill, f32)], axis=0).astype(f32)

    return full(cos64, 1.0), full(slo64, 0.0), full(shi64, 0.0)


def _pad_cols(w, n):
    return jnp.pad(w, ((0, 0), (0, n - w.shape[1])))


def kernel(x, c, ctx, c_ctx, ada_w, ada_b, norm_g, ffn_w_in, ffn_w_out, even_w_in, even_w_out, diff_lambda,
           diff_norm_g, mlstm_conv_w, mlstm_conv_b, mlstm_gate_b, mlstm_norm_g, odd_w_in, odd_w_out,
           gla_w_gate, gla_b_gate, gla_norm_g, final_g):
    bsz, seq, d = x.shape
    n_ctx = ctx.shape[1]
    depth = ada_w.shape[0]
    r_lat, r_ctx = bsz * seq, bsz * n_ctx
    tm = min(ROW_TILE, seq, r_ctx)
    assert seq % tm == 0 and r_ctx % tm == 0 and tm % ATT_VBLK == 0 and seq % ATT_TK == 0
    assert seq % ML_CHUNK == 0 and n_ctx % ML_CHUNK == 0 and n_ctx % ATT_TQ == 0 and r_lat % n_ctx == 0
    assert seq & (seq - 1) == 0 and n_ctx & (n_ctx - 1) == 0 and n_ctx % GLA_BLOCK == 0
    n_lat_tiles, per_batch = r_lat // tm, seq // tm
    lay = {
        "b": bsz, "seq": seq, "ctx": n_ctx, "tm": tm,
        "mod_index": lambda i: jnp.where(i < n_lat_tiles, 1 + i // per_batch, 0),
        "rope_index": lambda i: jnp.where(i < n_lat_tiles, i % per_batch, per_batch),
    }
    n_all = (r_lat + r_ctx) // tm

    n_s = -(-(bsz + 1) // 8) * 8
    svec = jnp.zeros((n_s, d), F32).at[:bsz].set(c).at[bsz].set(c_ctx)
    mods = _modulation(svec, ada_w, ada_b)
    mods = jnp.concatenate([mods[:, bsz:bsz + 1], mods[:, :bsz]], axis=1).reshape(depth, bsz + 1, N_MOD, d)

    rope = _rope_tables(seq, tm)
    xs, tail = x.reshape(r_lat, d), ctx.reshape(r_ctx, d)
    w_in_b = ffn_w_in.astype(BF16)
    w_out_b = ffn_w_out.astype(BF16)

    for l in range(depth):
        last = l == depth - 1
        mod = mods[l]
        xs = _ffn(xs, mod, norm_g[l, 0], w_in_b, w_out_b, (l, 0), final_g, lay=lay, tm=tm, n_rows=r_lat + r_ctx,
                  base=0, final=False, tail=tail if l == 0 else None)
        n_keep = r_lat if last else r_lat + r_ctx
        if l % 2 == 0:
            e = l // 2
            lam_init = 0.8 - 0.6 * math.exp(-0.3 * l)
            w = _pad_cols(even_w_in[e], 3200).astype(BF16)
            gate_b = _pad_cols(mlstm_gate_b[e].reshape(1, -1), LANES)
            q, k, vt, mlqk, mlv, opre, gates = _even_proj(xs, mod, norm_g[l, 1], w, gate_b, rope, lay=lay)
            a = _attention(q, k, vt, diff_lambda[e], diff_norm_g[e], lay=lay, lam_init=lam_init)
            qm, km = _ml_conv(mlqk, mlstm_conv_w[e], mlstm_conv_b[e], lay=lay)
            mf, mb = _mlstm(qm, km, mlv, gates, lay=lay)
            mix, mix_args = "even", (a, mf, mb, opre, mlstm_norm_g[e].reshape(1, 512), even_w_out[e].astype(BF16))
        else:
            o = l // 2
            w = _pad_cols(odd_w_in[o], 3200).astype(BF16)
            wg = jnp.zeros((LANES, 1024), F32)
            wg = wg.at[0:GLA_RANK, 0:512].set(gla_w_gate[o, 0]).at[GLA_RANK:2 * GLA_RANK, 512:1024].set(gla_w_gate[o, 1])
            bg = gla_b_gate[o].reshape(1, 1024)
            q, k, v, rgate, la_f, la_b = _odd_proj(xs, mod, norm_g[l, 1], w, wg.astype(BF16), bg, lay=lay)
            of, ob = _gla(q, k, v, la_f, la_b, lay=lay)
            mix, mix_args = "odd", (of, ob, rgate, jnp.tile(gla_norm_g[o], GLA_HEADS).reshape(1, 1024), odd_w_out[o].astype(BF16))
        xs = _ffn(xs, mod, norm_g[l, 2], w_in_b, w_out_b, (l, 1), final_g, lay=lay, tm=min(MIX_TILE, tm), n_rows=n_keep,
                  base=6, final=last, mix=mix, mix_args=mix_args)
    return xs[:r_lat].reshape(bsz, seq, d)
```

```python
import functools
import math

import jax
import jax.numpy as jnp
import numpy as np
from jax import lax
from jax.experimental import pallas as pl
from jax.experimental.pallas import tpu as pltpu

F32 = jnp.float32
BF16 = jnp.bfloat16

EPS = 1e-6
N_MOD = 9
GRID_W = 64
ROPE_BASE = 10000.0
DA_HEADS, DA_HD, DA_VD = 4, 64, 128
ML_HEADS, ML_QK, ML_V = 4, 64, 128
GLA_HEADS, GLA_K, GLA_V, GLA_RANK, GLA_TAU = 4, 128, 256, 16, 16.0

LANES = 128
MXU_DIM = 256
VMEM_LIMIT = 56 * 1024 * 1024
NEG = -1e30
LOG2E = math.log2(math.e)

ROW_TILE = 1024
FF_TILE = 256
MIX_TILE = 512
ATT_TQ = 256
ATT_TK = 2048
ATT_VBLK = 256
ML_CHUNK = 256
GLA_CHUNK = 128
GLA_BLOCK = 256
CONV_TILE = 1024


def _mod_index(lay, tm):
    n_lat, per_batch = lay["b"] * lay["seq"] // tm, lay["seq"] // tm
    return lambda i: jnp.where(i < n_lat, 1 + i // per_batch, 0)


def _cparams(sem):
    return pltpu.CompilerParams(dimension_semantics=sem, vmem_limit_bytes=VMEM_LIMIT)


def _sigmoid(x):
    return 1.0 / (1.0 + jnp.exp(-x))


def _silu(x):
    return x * _sigmoid(x)


def _log_sigmoid(x):
    return jnp.minimum(x, 0.0) - jnp.log1p(jnp.exp(-jnp.abs(x)))


def _rms(x, axis=-1):
    return x * lax.rsqrt(jnp.mean(x * x, axis=axis, keepdims=True) + EPS)


def _ada_norm(x, g, shift, scale):
    return _rms(x) * g * (1.0 + scale) + shift


def _dot(a, b):
    return jnp.dot(a, b, preferred_element_type=F32)


def _dot_nt(a, b):
    return lax.dot_general(a, b, (((1,), (1,)), ((), ())), preferred_element_type=F32)


def _dot_tn(a, b):
    return lax.dot_general(a, b, (((0,), (0,)), ((), ())), preferred_element_type=F32)


def _scan_lanes(x, reverse, op, fill):
    n = x.shape[1]
    lane = lax.broadcasted_iota(jnp.int32, x.shape, 1)
    k = 1
    while k < n:
        if reverse:
            x = op(x, jnp.where(lane < n - k, pltpu.roll(x, n - k, 1), fill))
        else:
            x = op(x, jnp.where(lane >= k, pltpu.roll(x, k, 1), fill))
        k *= 2
    return x


def _tri_cumsum(tri, x):
    if x.dtype == BF16:
        return _dot(tri, x)
    hi = x.astype(BF16)
    lo = (x - hi.astype(F32)).astype(BF16)
    r = _dot(tri, jnp.concatenate([hi, lo], axis=1))
    w = x.shape[1]
    return r[:, 0:w] + r[:, w:2 * w]


def _mod_kernel(s_ref, w_ref, b_ref, o_ref):
    s = _silu(s_ref[...])
    o_ref[0] = jnp.dot(s, w_ref[0], preferred_element_type=F32,
                       precision=lax.Precision.HIGHEST) + b_ref[0]


def _modulation(svec, ada_w, ada_b):
    depth, d, n = ada_w.shape
    tn = d
    return pl.pallas_call(
        _mod_kernel,
        grid=(depth, n // tn),
        in_specs=[pl.BlockSpec(svec.shape, lambda l, j: (0, 0)),
                  pl.BlockSpec((1, d, tn), lambda l, j: (l, 0, j)),
                  pl.BlockSpec((1, 1, tn), lambda l, j: (l, 0, j))],
        out_specs=pl.BlockSpec((1, svec.shape[0], tn), lambda l, j: (l, 0, j)),
        out_shape=jax.ShapeDtypeStruct((depth, svec.shape[0], n), F32),
        compiler_params=_cparams(("parallel", "parallel")),
    )(svec, ada_w, ada_b.reshape(depth, 1, n))


def _group_rms(x, width):
    outs = []
    for h in range(x.shape[1] // width):
        outs.append(_rms(x[:, h * width:(h + 1) * width]))
    return jnp.concatenate(outs, axis=1)


def _ffn_kernel(*refs, base, final, n_first, mix):
    x_ref, c_ref, mod_ref, g_ref, wi_ref, wo_ref, fg_ref = refs[:7]
    o_ref, u_ref = refs[-2:]
    dff = wo_ref.shape[0]
    x = x_ref[...]
    if n_first is not None:
        x = jnp.where(pl.program_id(0) < n_first, x, c_ref[...])
    if mix == "even":
        a_ref, mf_ref, mb_ref, op_ref, gm_ref, wm_ref = refs[7:13]
        mem = mf_ref[...].astype(F32) + mb_ref[...].astype(F32)
        m = _group_rms(mem, ML_V) * gm_ref[...] * _sigmoid(op_ref[...].astype(F32))
        y = _dot(a_ref[...], wm_ref[0:512, :]) + _dot(m.astype(BF16), wm_ref[512:1024, :])
        x = x + mod_ref[0, 5:6, :] * y
    elif mix == "odd":
        of_ref, ob_ref, r_ref, gm_ref, wm_ref = refs[7:12]
        o = of_ref[...].astype(F32) + ob_ref[...].astype(F32)
        y = _group_rms(o, GLA_V) * gm_ref[...] * _silu(r_ref[...].astype(F32))
        x = x + mod_ref[0, 5:6, :] * _dot(y.astype(BF16), wm_ref[...])
    h = _ada_norm(x, g_ref[...], mod_ref[0, base:base + 1, :], mod_ref[0, base + 1:base + 2, :]).astype(BF16)
    for j in range(dff // FF_TILE):
        a = _dot(h, wi_ref[:, j * FF_TILE:(j + 1) * FF_TILE])
        b = _dot(h, wi_ref[:, dff + j * FF_TILE:dff + (j + 1) * FF_TILE])
        u_ref[:, j * FF_TILE:(j + 1) * FF_TILE] = (_silu(a) * b).astype(BF16)
    y = x + 0.5 * mod_ref[0, base + 2:base + 3, :] * _dot(u_ref[...], wo_ref[...])
    if final:
        y = _rms(y) * fg_ref[...]
    o_ref[...] = y


def _ffn(xs, mod, g, w_in, w_out, widx, final_g, *, lay, tm, n_rows, base, final, tail=None, mix=None, mix_args=()):
    d = xs.shape[1]
    dff = w_out.shape[2]
    n_tiles = n_rows // tm
    wmap = lambda i: widx + (0, 0)
    midx = _mod_index(lay, tm)
    const = lambda i: (0, 0)
    row = lambda i: (i, 0)
    once = pl.Buffered(1)
    if tail is None:
        n_first, tail, xmap, cspec = None, xs, row, pl.BlockSpec((8, d), const)
    else:
        n_first = xs.shape[0] // tm
        xmap = lambda i: (jnp.minimum(i, n_first - 1), 0)
        cspec = pl.BlockSpec((tm, d), lambda i: (jnp.maximum(i - n_first, 0), 0))
    mix_specs = []
    for arr in mix_args:
        if arr.shape[0] == 1:
            mix_specs.append(pl.BlockSpec(arr.shape, const))
        elif arr.shape[0] == d and arr.shape[1] == d:
            mix_specs.append(pl.BlockSpec(arr.shape, const, pipeline_mode=once))
        else:
            mix_specs.append(pl.BlockSpec((tm, arr.shape[1]), row))
    return pl.pallas_call(
        functools.partial(_ffn_kernel, base=base, final=final, n_first=n_first, mix=mix),
        grid=(n_tiles,),
        in_specs=[pl.BlockSpec((tm, d), xmap),
                  cspec,
                  pl.BlockSpec((1, N_MOD, d), lambda i: (midx(i), 0, 0)),
                  pl.BlockSpec((1, d), const),
                  pl.BlockSpec((None, None) + w_in.shape[2:], wmap, pipeline_mode=once),
                  pl.BlockSpec((None, None) + w_out.shape[2:], wmap, pipeline_mode=once),
                  pl.BlockSpec((1, d), const)] + mix_specs,
        out_specs=pl.BlockSpec((tm, d), row),
        out_shape=jax.ShapeDtypeStruct((n_rows, d), F32),
        scratch_shapes=[pltpu.VMEM((tm, dff), BF16)],
        compiler_params=_cparams(("parallel",)),
    )(xs, tail, mod, g.reshape(1, d), w_in, w_out, final_g.reshape(1, d), *mix_args)


def _rope_heads(r, c, slo, shi, scale):
    outs = []
    for h in range(DA_HEADS):
        rh = r[:, h * LANES:(h + 1) * LANES]
        y = rh * c + pltpu.roll(rh, LANES - 16, 1) * slo + pltpu.roll(rh, 16, 1) * shi
        outs.append(y * scale if scale != 1.0 else y)
    return jnp.concatenate(outs, axis=1)


def _even_proj_kernel(x_ref, mod_ref, g_ref, w_ref, gb_ref, c_ref, slo_ref, shi_ref,
                      q_ref, k_ref, vt_ref, mlqk_ref, mlv_ref, opre_ref, gate_ref):
    h = _ada_norm(x_ref[...], g_ref[...], mod_ref[0, 3:4, :], mod_ref[0, 4:5, :]).astype(BF16)
    c, slo, shi = c_ref[...], slo_ref[...], shi_ref[...]
    q_ref[...] = _rope_heads(_dot(h, w_ref[:, 0:512]), c, slo, shi, DA_HD ** -0.5 * LOG2E).astype(BF16)
    k_ref[...] = _rope_heads(_dot(h, w_ref[:, 512:1024]), c, slo, shi, 1.0).astype(BF16)
    vt = _dot(h, w_ref[:, 1024:1536]).T.astype(BF16)
    for i in range(vt_ref.shape[0]):
        vt_ref[i] = vt[:, i * ATT_VBLK:(i + 1) * ATT_VBLK]
    mlqk_ref[...] = _dot(h, w_ref[:, 1536:2048])
    mlv_ref[...] = _dot(h, w_ref[:, 2048:2560]).astype(BF16)
    opre_ref[...] = _dot(h, w_ref[:, 2560:3072]).astype(BF16)
    gate_ref[...] = _dot(h, w_ref[:, 3072:3200]) + gb_ref[...]


def _even_proj(xs, mod, g, w, gate_b, rope, *, lay):
    r, d = xs.shape
    tm = lay["tm"]
    n_tiles = r // tm
    midx, ridx = lay["mod_index"], lay["rope_index"]
    row = lambda i: (i, 0)
    const = lambda i: (0, 0)
    tab = pl.BlockSpec((tm, LANES), lambda i: (ridx(i), 0))
    return pl.pallas_call(
        _even_proj_kernel,
        grid=(n_tiles,),
        in_specs=[pl.BlockSpec((tm, d), row),
                  pl.BlockSpec((1, N_MOD, d), lambda i: (midx(i), 0, 0)),
                  pl.BlockSpec((1, d), const),
                  pl.BlockSpec(w.shape, const),
                  pl.BlockSpec((1, LANES), const),
                  tab, tab, tab],
        out_specs=[pl.BlockSpec((tm, 512), row), pl.BlockSpec((tm, 512), row),
                   pl.BlockSpec((tm // ATT_VBLK, 512, ATT_VBLK), lambda i: (i, 0, 0)),
                   pl.BlockSpec((tm, 512), row), pl.BlockSpec((tm, 512), row),
                   pl.BlockSpec((tm, 512), row), pl.BlockSpec((tm, LANES), row)],
        out_shape=[jax.ShapeDtypeStruct((r, 512), BF16), jax.ShapeDtypeStruct((r, 512), BF16),
                   jax.ShapeDtypeStruct((r // ATT_VBLK, 512, ATT_VBLK), BF16),
                   jax.ShapeDtypeStruct((r, 512), F32), jax.ShapeDtypeStruct((r, 512), BF16),
                   jax.ShapeDtypeStruct((r, 512), BF16), jax.ShapeDtypeStruct((r, LANES), F32)],
        compiler_params=_cparams(("parallel",)),
    )(xs, mod, g.reshape(1, d), w, gate_b, *rope)


def _attn_kernel(lam_ref, g_ref, q_ref, kl_ref, kc_ref, vl_ref, vc_ref, o_ref, acc_ref, m_ref, l_ref,
                 *, nql, n_lat_chunks, n_ctx_blocks, lam_init):
    t = pl.program_id(1)
    tq, vb = ATT_TQ, ATT_VBLK
    per = ATT_TK // vb
    m_ref[...] = jnp.full(m_ref.shape, NEG, F32)
    l_ref[...] = jnp.zeros_like(l_ref)
    acc_ref[...] = jnp.zeros_like(acc_ref)

    q = q_ref[...]
    lane = lax.broadcasted_iota(jnp.int32, (tq, LANES), 1)
    qcat = []
    for h in range(DA_HEADS):
        qh = q[:, h * LANES:(h + 1) * LANES]
        qcat.append(jnp.concatenate([jnp.where(lane < DA_HD, qh, jnp.zeros_like(qh)),
                                     jnp.where(lane >= DA_HD, qh, jnp.zeros_like(qh))], axis=0))

    def block(k_of, v_of):
        ss = [_dot_nt(k_of(h), qcat[h]) for h in range(DA_HEADS)]
        alphas, ps = [], []
        for h in range(DA_HEADS):
            m_old = m_ref[h]
            m_new = jnp.maximum(m_old, jnp.max(ss[h], axis=0, keepdims=True))
            alpha = jnp.exp2(m_old - m_new)
            p = jnp.exp2(ss[h] - m_new)
            l_ref[h] = alpha * l_ref[h] + jnp.sum(p, axis=0, keepdims=True)
            m_ref[h] = m_new
            alphas.append(alpha)
            ps.append(p.astype(BF16))
        for h in range(DA_HEADS):
            acc_ref[h] = alphas[h] * acc_ref[h] + _dot(v_of(h), ps[h])

    hs = [slice(h * LANES, (h + 1) * LANES) for h in range(DA_HEADS)]

    @pl.when(t >= nql)
    def _():
        block(lambda h: kc_ref[:, hs[h]],
              lambda h: jnp.concatenate([vc_ref[i, hs[h], :] for i in range(n_ctx_blocks)], axis=1))

    @pl.when(t < nql)
    def _():
        for j in range(n_lat_chunks):
            rows = slice(j * ATT_TK, (j + 1) * ATT_TK)
            vparts = lambda h, j=j: [vl_ref[j * per + i, hs[h], :] for i in range(per)]
            if j == 0:
                block(lambda h: jnp.concatenate([kc_ref[:, hs[h]], kl_ref[rows, hs[h]]], axis=0),
                      lambda h: jnp.concatenate([vc_ref[i, hs[h], :] for i in range(n_ctx_blocks)] + vparts(h), axis=1))
            else:
                block(lambda h, rows=rows: kl_ref[rows, hs[h]], lambda h: jnp.concatenate(vparts(h), axis=1))

    lp = lam_ref[...]
    lam = (jnp.exp(jnp.sum(lp[0:1] * lp[1:2], axis=1, keepdims=True))
           - jnp.exp(jnp.sum(lp[2:3] * lp[3:4], axis=1, keepdims=True)) + lam_init)
    g = g_ref[...] * (1.0 - lam_init)
    for h in range(DA_HEADS):
        on = acc_ref[h] / l_ref[h]
        y = _rms(on[:, 0:tq] - lam * on[:, tq:2 * tq], axis=0) * g
        o_ref[:, h * LANES:(h + 1) * LANES] = y.T.astype(BF16)


def _attention(q, k, vt, diff_lambda, diff_norm_g, *, lay, lam_init):
    bsz, seq, ctx = lay["b"], lay["seq"], lay["ctx"]
    r_lat = bsz * seq
    tq, vb = ATT_TQ, ATT_VBLK
    nql, nqc, nkl, nkc = seq // tq, ctx // tq, seq // vb, ctx // vb
    nlq = r_lat // tq
    nctx0 = r_lat // ctx

    def qidx(b, t):
        return (jnp.where(t < nql, b * nql + t, nlq + b * nqc + (t - nql)), 0)

    return pl.pallas_call(
        functools.partial(_attn_kernel, nql=nql, n_lat_chunks=seq // ATT_TK, n_ctx_blocks=nkc, lam_init=lam_init),
        grid=(bsz, nql + nqc),
        in_specs=[pl.BlockSpec((4, DA_HD), lambda b, t: (0, 0)),
                  pl.BlockSpec((DA_VD, 1), lambda b, t: (0, 0)),
                  pl.BlockSpec((tq, 512), qidx),
                  pl.BlockSpec((seq, 512), lambda b, t: (b, 0)),
                  pl.BlockSpec((ctx, 512), lambda b, t: (nctx0 + b, 0)),
                  pl.BlockSpec((nkl, 512, vb), lambda b, t: (b, 0, 0)),
                  pl.BlockSpec((nkc, 512, vb), lambda b, t: (nctx0 + b, 0, 0))],
        out_specs=pl.BlockSpec((tq, 512), qidx),
        out_shape=jax.ShapeDtypeStruct(q.shape, BF16),
        scratch_shapes=[pltpu.VMEM((DA_HEADS, DA_VD, 2 * tq), F32), pltpu.VMEM((DA_HEADS, 1, 2 * tq), F32),
                        pltpu.VMEM((DA_HEADS, 1, 2 * tq), F32)],
        compiler_params=_cparams(("parallel", "parallel")),
    )(diff_lambda, diff_norm_g.reshape(DA_VD, 1), q, k, k, vt, vt)


def _conv_kernel(x_ref, prev_ref, next_ref, w_ref, b_ref, q_ref, k_ref, *, lay):
    i = pl.program_id(0)
    tc = x_ref.shape[0]
    seq_len = jnp.where(i < lay["b"] * lay["seq"] // tc, lay["seq"], lay["ctx"])
    x = x_ref[...]
    rows = lax.broadcasted_iota(jnp.int32, x.shape, 0)
    pos = (i * tc + rows) & (seq_len - 1)
    up = jnp.where(rows == 0, prev_ref[7:8, :], pltpu.roll(x, 1, 0))
    dn = jnp.where(rows == tc - 1, next_ref[0:1, :], pltpu.roll(x, tc - 1, 0))
    up = jnp.where(pos == 0, 0.0, up)
    dn = jnp.where(pos == seq_len - 1, 0.0, dn)
    y = _silu(w_ref[0:1, :] * up + w_ref[1:2, :] * x + w_ref[2:3, :] * dn + b_ref[...])
    q_ref[...] = (y[:, 0:256] * (ML_QK ** -0.5)).astype(BF16)
    k_ref[...] = y[:, 256:512].astype(BF16)


def _ml_conv(mlqk, conv_w, conv_b, *, lay):
    r = mlqk.shape[0]
    tc = min(CONV_TILE, lay["tm"])
    n = r // tc
    sub = tc // 8
    last = r // 8 - 1
    return pl.pallas_call(
        functools.partial(_conv_kernel, lay=lay),
        grid=(n,),
        in_specs=[pl.BlockSpec((tc, 512), lambda i: (i, 0)),
                  pl.BlockSpec((8, 512), lambda i: (jnp.maximum(i * sub - 1, 0), 0)),
                  pl.BlockSpec((8, 512), lambda i: (jnp.minimum((i + 1) * sub, last), 0)),
                  pl.BlockSpec((3, 512), lambda i: (0, 0)),
                  pl.BlockSpec((1, 512), lambda i: (0, 0))],
        out_specs=[pl.BlockSpec((tc, 256), lambda i: (i, 0)), pl.BlockSpec((tc, 256), lambda i: (i, 0))],
        out_shape=[jax.ShapeDtypeStruct((r, 256), BF16), jax.ShapeDtypeStruct((r, 256), BF16)],
        compiler_params=_cparams(("parallel",)),
    )(mlqk, mlqk, mlqk, conv_w, conv_b.reshape(1, 512))


def _scan_indices(lay, chunk):
    bsz, seq, ctx = lay["b"], lay["seq"], lay["ctx"]
    nc_lat, nc_ctx = seq // chunk, ctx // chunk
    nl = bsz * nc_lat

    def fwd(b, j):
        return jnp.where(j < nc_ctx, nl + b * nc_ctx + j, b * nc_lat + (j - nc_ctx))

    def bwd(b, j):
        return jnp.where(j < nc_ctx, nl + b * nc_ctx + (nc_ctx - 1 - j), b * nc_lat + (nc_lat - 1 - (j - nc_ctx)))

    return fwd, bwd, nc_ctx + nc_lat


def _mlstm_step(dirs, ctn_ref, m_ref):
    n = dirs[0][0].shape[0]
    rows = lax.broadcasted_iota(jnp.int32, (n, n), 0)
    cols = lax.broadcasted_iota(jnp.int32, (n, n), 1)
    lane_q = lax.broadcasted_iota(jnp.int32, (n, ML_HEADS * ML_QK), 1)
    heads = [(lane_q >= h * ML_QK) & (lane_q < (h + 1) * ML_QK) for h in range(ML_HEADS)]
    pairs = [(d, h) for d in range(2) for h in range(ML_HEADS)]
    ones = jnp.ones((n, LANES), BF16)

    pre = []
    for d, (q, k, v, gates, _) in enumerate(dirs):
        gt = gates.T
        edge = 0 if d == 1 else n - 1
        bc = _scan_lanes(_log_sigmoid(gt[8:16, :]), d == 1, jnp.add, 0.0)
        cp = gt[0:8, :] - bc
        m_old = m_ref[d][:, 0:1]
        mx = jnp.maximum(m_old, _scan_lanes(cp, d == 1, jnp.maximum, NEG))
        mx_end = mx[:, edge:edge + 1]
        m_ref[d] = jnp.broadcast_to(bc[:, edge:edge + 1] + mx_end, (8, LANES))
        stack = jnp.concatenate([mx, bc + mx, jnp.exp(cp - mx_end), jnp.zeros((LANES - 24, n), F32)], axis=0)
        pre.append((cp, m_old, jnp.exp(m_old - mx_end), stack.T, ctn_ref[d]))

    qhs, s_raw, r2 = {}, {}, {}
    for d, h in pairs:
        q = dirs[d][0]
        qhs[d, h] = jnp.where(heads[h], q, jnp.zeros_like(q))
    for d, h in pairs:
        s_raw[d, h] = _dot_nt(qhs[d, h], dirs[d][1])
    for d, h in pairs:
        r2[d, h] = _dot(qhs[d, h], pre[d][4].astype(BF16))

    p = {}
    for d, h in pairs:
        cp, _, _, colf, _ = pre[d]
        r = d * ML_HEADS + h
        mask = (cols >= rows) if d == 1 else (cols <= rows)
        mxb = jnp.broadcast_to(colf[:, r:r + 1], (n, LANES))
        dm = jnp.where(mask, cp[r:r + 1, :] - jnp.concatenate([mxb] * (n // LANES), axis=1), NEG)
        p[d, h] = (s_raw[d, h] * jnp.exp(dm)).astype(BF16)

    nd = {}
    for d, h in pairs:
        v = dirs[d][2]
        nd[d, h] = _dot(p[d, h], jnp.concatenate([v[:, h * ML_V:(h + 1) * ML_V], ones], axis=1))

    for d, h in pairs:
        _, m_old, _, colf, _ = pre[d]
        r = d * ML_HEADS + h
        w = jnp.exp(m_old[r:r + 1, :] - jnp.broadcast_to(colf[:, r:r + 1], (n, LANES)))
        floor = jnp.exp(-jnp.broadcast_to(colf[:, 8 + r:9 + r], (n, LANES)))
        tot = nd[d, h] + jnp.concatenate([w, w], axis=1) * r2[d, h]
        hid = tot[:, 0:ML_V] / jnp.maximum(jnp.abs(tot[:, ML_V:2 * ML_V]), floor)
        dirs[d][4][:, h * ML_V:(h + 1) * ML_V] = hid.astype(BF16)

    for d, (q, k, v, gates, _) in enumerate(dirs):
        _, _, decay, colf, ctn = pre[d]
        wk = jnp.zeros((n, ML_HEADS * ML_QK), F32)
        for h in range(ML_HEADS):
            r = d * ML_HEADS + h
            wk = jnp.where(heads[h], colf[:, 16 + r:17 + r], wk)
        kw = (k.astype(F32) * wk).astype(BF16)
        u = _dot_tn(kw, jnp.concatenate([v, ones], axis=1))
        for h in range(ML_HEADS):
            r0, r = h * ML_QK, d * ML_HEADS + h
            dec = decay[r:r + 1, :]
            ctn_ref[d, r0:r0 + ML_QK, 0:ML_V] = dec * ctn[r0:r0 + ML_QK, 0:ML_V] + u[r0:r0 + ML_QK, h * ML_V:(h + 1) * ML_V]
            ctn_ref[d, r0:r0 + ML_QK, ML_V:2 * ML_V] = (dec * ctn[r0:r0 + ML_QK, ML_V:2 * ML_V]
                                                        + u[r0:r0 + ML_QK, 4 * ML_V:5 * ML_V])


def _mlstm_kernel(qf_ref, kf_ref, vf_ref, gf_ref, qb_ref, kb_ref, vb_ref, gb_ref, of_ref, ob_ref, ctn_ref, m_ref):
    @pl.when(pl.program_id(1) == 0)
    def _():
        ctn_ref[...] = jnp.zeros_like(ctn_ref)
        m_ref[...] = jnp.zeros_like(m_ref)

    _mlstm_step([(qf_ref[...], kf_ref[...], vf_ref[...], gf_ref[...], of_ref),
                 (qb_ref[...], kb_ref[...], vb_ref[...], gb_ref[...], ob_ref)], ctn_ref, m_ref)


def _mlstm(q, k, v, gates, *, lay):
    n = ML_CHUNK
    fwd, bwd, steps = _scan_indices(lay, n)
    r = q.shape[0]

    def specs(idx):
        return [pl.BlockSpec((n, 256), lambda b, j: (idx(b, j), 0)),
                pl.BlockSpec((n, 256), lambda b, j: (idx(b, j), 0)),
                pl.BlockSpec((n, 512), lambda b, j: (idx(b, j), 0)),
                pl.BlockSpec((n, LANES), lambda b, j: (idx(b, j), 0))]

    return pl.pallas_call(
        _mlstm_kernel,
        grid=(lay["b"], steps),
        in_specs=specs(fwd) + specs(bwd),
        out_specs=[pl.BlockSpec((n, 512), lambda b, j: (fwd(b, j), 0)),
                   pl.BlockSpec((n, 512), lambda b, j: (bwd(b, j), 0))],
        out_shape=[jax.ShapeDtypeStruct((r, 512), BF16), jax.ShapeDtypeStruct((r, 512), BF16)],
        scratch_shapes=[pltpu.VMEM((2, 256, 256), F32), pltpu.VMEM((2, 8, LANES), F32)],
        compiler_params=_cparams(("parallel", "arbitrary")),
    )(q, k, v, gates, q, k, v, gates)


def _gla_step(dirs, st_ref):
    n = dirs[0][0].shape[0]
    rows = lax.broadcasted_iota(jnp.int32, (n, n), 0)
    cols = lax.broadcasted_iota(jnp.int32, (n, n), 1)
    mid = n // 2
    pairs = [(d, h) for d in range(2) for h in range(GLA_HEADS)]
    ks = [slice(h * GLA_K, (h + 1) * GLA_K) for h in range(GLA_HEADS)]
    vs = [slice(h * GLA_V, (h + 1) * GLA_V) for h in range(GLA_HEADS)]

    masks = [cols <= rows, cols >= rows]
    bcs = [_tri_cumsum(jnp.where(masks[d], 1.0, 0.0).astype(BF16), dirs[d][3]) for d in range(2)]

    qe, ke, qs, kd, dec, sts = {}, {}, {}, {}, {}, {}
    for d, h in pairs:
        q, k, _, la, _ = dirs[d]
        edge = 0 if d == 1 else n - 1
        bc = bcs[d][:, ks[h]]
        bmid, bl = bc[mid:mid + 1, :], bc[edge:edge + 1, :]
        e_up, e_dn = jnp.exp(bc - bmid), jnp.exp(bmid - bc)
        qf, kf = q[:, ks[h]].astype(F32), k[:, ks[h]].astype(F32)
        qe[d, h] = (qf * e_up).astype(BF16)
        ke[d, h] = (kf * e_dn).astype(BF16)
        qs[d, h] = (qf * (e_up * jnp.exp(bmid))).astype(BF16)
        kd[d, h] = (kf * (e_dn * jnp.exp(bl - bmid))).astype(BF16)
        dec[d, h] = jnp.exp(bl)
        sts[d, h] = st_ref[d, h]

    a, inter = {}, {}
    for d, h in pairs:
        a[d, h] = _dot_nt(qe[d, h], ke[d, h])
    for d, h in pairs:
        inter[d, h] = _dot_nt(qs[d, h], sts[d, h].astype(BF16))
    for d, h in pairs:
        a[d, h] = jnp.where(masks[d], a[d, h], 0.0).astype(BF16)
    for d, h in pairs:
        dirs[d][4][:, vs[h]] = (_dot(a[d, h], dirs[d][2][:, vs[h]]) + inter[d, h]).astype(BF16)
    for d, h in pairs:
        st_ref[d, h] = sts[d, h] * dec[d, h] + _dot_tn(dirs[d][2][:, vs[h]], kd[d, h])


def _gla_kernel(qf_ref, kf_ref, vf_ref, af_ref, qb_ref, kb_ref, vb_ref, ab_ref, of_ref, ob_ref, st_ref):
    @pl.when(pl.program_id(1) == 0)
    def _():
        st_ref[...] = jnp.zeros_like(st_ref)

    n = GLA_CHUNK
    per = qf_ref.shape[0] // n
    for c in range(per):
        f, b = pl.ds(c * n, n), pl.ds((per - 1 - c) * n, n)
        _gla_step([(qf_ref[f, :], kf_ref[f, :], vf_ref[f, :], af_ref[f, :], of_ref.at[f]),
                   (qb_ref[b, :], kb_ref[b, :], vb_ref[b, :], ab_ref[b, :], ob_ref.at[b])], st_ref)


def _gla(q, k, v, la_f, la_b, *, lay):
    n = GLA_BLOCK
    fwd, bwd, steps = _scan_indices(lay, n)
    r = q.shape[0]

    def specs(idx):
        return [pl.BlockSpec((n, 512), lambda b, j: (idx(b, j), 0)),
                pl.BlockSpec((n, 512), lambda b, j: (idx(b, j), 0)),
                pl.BlockSpec((n, 1024), lambda b, j: (idx(b, j), 0)),
                pl.BlockSpec((n, 512), lambda b, j: (idx(b, j), 0))]

    return pl.pallas_call(
        _gla_kernel,
        grid=(lay["b"], steps),
        in_specs=specs(fwd) + specs(bwd),
        out_specs=[pl.BlockSpec((n, 1024), lambda b, j: (fwd(b, j), 0)),
                   pl.BlockSpec((n, 1024), lambda b, j: (bwd(b, j), 0))],
        out_shape=[jax.ShapeDtypeStruct((r, 1024), BF16), jax.ShapeDtypeStruct((r, 1024), BF16)],
        scratch_shapes=[pltpu.VMEM((2, GLA_HEADS, GLA_V, GLA_K), F32)],
        compiler_params=_cparams(("parallel", "arbitrary")),
    )(q, k, v, la_f, q, k, v, la_b)


def _odd_proj_kernel(x_ref, mod_ref, g_ref, w_ref, wg_ref, bg_ref, q_ref, k_ref, v_ref, r_ref, laf_ref, lab_ref):
    h = _ada_norm(x_ref[...], g_ref[...], mod_ref[0, 3:4, :], mod_ref[0, 4:5, :]).astype(BF16)
    q_ref[...] = (_dot(h, w_ref[:, 0:512]) * (GLA_K ** -0.5)).astype(BF16)
    k_ref[...] = _dot(h, w_ref[:, 512:1024]).astype(BF16)
    v_ref[...] = _dot(h, w_ref[:, 1024:2048]).astype(BF16)
    r_ref[...] = _dot(h, w_ref[:, 2048:3072]).astype(BF16)
    lr = _dot(h, w_ref[:, 3072:3200]).astype(BF16)
    la = _log_sigmoid(_dot(lr, wg_ref[...]) + bg_ref[...]) * (1.0 / GLA_TAU)
    laf_ref[...] = la[:, 0:512].astype(BF16)
    lab_ref[...] = la[:, 512:1024].astype(BF16)


def _odd_proj(xs, mod, g, w, wg, bg, *, lay):
    r, d = xs.shape
    tm = lay["tm"]
    midx = lay["mod_index"]
    row = lambda i: (i, 0)
    const = lambda i: (0, 0)
    return pl.pallas_call(
        _odd_proj_kernel,
        grid=(r // tm,),
        in_specs=[pl.BlockSpec((tm, d), row),
                  pl.BlockSpec((1, N_MOD, d), lambda i: (midx(i), 0, 0)),
                  pl.BlockSpec((1, d), const), pl.BlockSpec(w.shape, const),
                  pl.BlockSpec(wg.shape, const), pl.BlockSpec((1, 1024), const)],
        out_specs=[pl.BlockSpec((tm, 512), row), pl.BlockSpec((tm, 512), row), pl.BlockSpec((tm, 1024), row),
                   pl.BlockSpec((tm, 1024), row), pl.BlockSpec((tm, 512), row), pl.BlockSpec((tm, 512), row)],
        out_shape=[jax.ShapeDtypeStruct((r, 512), BF16), jax.ShapeDtypeStruct((r, 512), BF16),
                   jax.ShapeDtypeStruct((r, 1024), BF16), jax.ShapeDtypeStruct((r, 1024), BF16),
                   jax.ShapeDtypeStruct((r, 512), BF16), jax.ShapeDtypeStruct((r, 512), BF16)],
        compiler_params=_cparams(("parallel",)),
    )(xs, mod, g.reshape(1, d), w, wg, bg)


def _rope_tables(seq, tm):
    half = DA_HD // 4
    f32 = np.float32
    row = np.repeat(np.arange(seq // GRID_W), GRID_W).astype(f32)
    col = (np.arange(seq) % GRID_W).astype(f32)
    inv = (f32(ROPE_BASE) ** (-np.arange(half, dtype=f32) * f32(2.0) / f32(2 * half))).astype(f32)
    ar, ac = row[:, None] * inv, col[:, None] * inv
    cos64 = np.concatenate([np.cos(ar), np.cos(ar), np.cos(ac), np.cos(ac)], axis=1)
    sin_r, sin_c, z = np.sin(ar), np.sin(ac), np.zeros_like(ar)
    slo64 = np.concatenate([-sin_r, z, -sin_c, z], axis=1)
    shi64 = np.concatenate([z, sin_r, z, sin_c], axis=1)

    def full(t, fill):
        t = np.concatenate([t, t], axis=1)
        return np.concatenate([t, np.full((tm, LANES), fill, f32)], axis=0).astype(f32)

    return full(cos64, 1.0), full(slo64, 0.0), full(shi64, 0.0)


def _pad_cols(w, n):
    return jnp.pad(w, ((0, 0), (0, n - w.shape[1])))


def kernel(x, c, ctx, c_ctx, ada_w, ada_b, norm_g, ffn_w_in, ffn_w_out, even_w_in, even_w_out, diff_lambda,
           diff_norm_g, mlstm_conv_w, mlstm_conv_b, mlstm_gate_b, mlstm_norm_g, odd_w_in, odd_w_out,
           gla_w_gate, gla_b_gate, gla_norm_g, final_g):
    bsz, seq, d = x.shape
    n_ctx = ctx.shape[1]
    depth = ada_w.shape[0]
    r_lat, r_ctx = bsz * seq, bsz * n_ctx
    tm = min(ROW_TILE, seq, r_ctx)
    assert seq % tm == 0 and r_ctx % tm == 0 and tm % ATT_VBLK == 0 and seq % ATT_TK == 0
    assert seq % ML_CHUNK == 0 and n_ctx % ML_CHUNK == 0 and n_ctx % ATT_TQ == 0 and r_lat % n_ctx == 0
    assert seq & (seq - 1) == 0 and n_ctx & (n_ctx - 1) == 0 and n_ctx % GLA_BLOCK == 0
    n_lat_tiles, per_batch = r_lat // tm, seq // tm
    lay = {
        "b": bsz, "seq": seq, "ctx": n_ctx, "tm": tm,
        "mod_index": lambda i: jnp.where(i < n_lat_tiles, 1 + i // per_batch, 0),
        "rope_index": lambda i: jnp.where(i < n_lat_tiles, i % per_batch, per_batch),
    }
    n_all = (r_lat + r_ctx) // tm

    n_s = -(-(bsz + 1) // 8) * 8
    svec = jnp.zeros((n_s, d), F32).at[:bsz].set(c).at[bsz].set(c_ctx)
    mods = _modulation(svec, ada_w, ada_b)
    mods = jnp.concatenate([mods[:, bsz:bsz + 1], mods[:, :bsz]], axis=1).reshape(depth, bsz + 1, N_MOD, d)

    rope = _rope_tables(seq, tm)
    xs, tail = x.reshape(r_lat, d), ctx.reshape(r_ctx, d)
    w_in_b = ffn_w_in.astype(BF16)
    w_out_b = ffn_w_out.astype(BF16)

    for l in range(depth):
        last = l == depth - 1
        mod = mods[l]
        xs = _ffn(xs, mod, norm_g[l, 0], w_in_b, w_out_b, (l, 0), final_g, lay=lay, tm=tm, n_rows=r_lat + r_ctx,
                  base=0, final=False, tail=tail if l == 0 else None)
        n_keep = r_lat if last else r_lat + r_ctx
        if l % 2 == 0:
            e = l // 2
            lam_init = 0.8 - 0.6 * math.exp(-0.3 * l)
            w = _pad_cols(even_w_in[e], 3200).astype(BF16)
            gate_b = _pad_cols(mlstm_gate_b[e].reshape(1, -1), LANES)
            q, k, vt, mlqk, mlv, opre, gates = _even_proj(xs, mod, norm_g[l, 1], w, gate_b, rope, lay=lay)
            a = _attention(q, k, vt, diff_lambda[e], diff_norm_g[e], lay=lay, lam_init=lam_init)
            qm, km = _ml_conv(mlqk, mlstm_conv_w[e], mlstm_conv_b[e], lay=lay)
            mf, mb = _mlstm(qm, km, mlv, gates, lay=lay)
            mix, mix_args = "even", (a, mf, mb, opre, mlstm_norm_g[e].reshape(1, 512), even_w_out[e].astype(BF16))
        else:
            o = l // 2
            w = _pad_cols(odd_w_in[o], 3200).astype(BF16)
            wg = jnp.zeros((LANES, 1024), F32)
            wg = wg.at[0:GLA_RANK, 0:512].set(gla_w_gate[o, 0]).at[GLA_RANK:2 * GLA_RANK, 512:1024].set(gla_w_gate[o, 1])
            bg = gla_b_gate[o].reshape(1, 1024)
            q, k, v, rgate, la_f, la_b = _odd_proj(xs, mod, norm_g[l, 1], w, wg.astype(BF16), bg, lay=lay)
            of, ob = _gla(q, k, v, la_f, la_b, lay=lay)
            mix, mix_args = "odd", (of, ob, rgate, jnp.tile(gla_norm_g[o], GLA_HEADS).reshape(1, 1024), odd_w_out[o].astype(BF16))
        xs = _ffn(xs, mod, norm_g[l, 2], w_in_b, w_out_b, (l, 1), final_g, lay=lay, tm=min(MIX_TILE, tm), n_rows=n_keep,
                  base=6, final=last, mix=mix, mix_args=mix_args)
    return xs[:r_lat].reshape(bsz, seq, d)
```

```python
import functools
import math

import jax
import jax.numpy as jnp
import numpy as np
from jax import lax
from jax.experimental import pallas as pl
from jax.experimental.pallas import tpu as pltpu

F32 = jnp.float32
BF16 = jnp.bfloat16

EPS = 1e-6
N_MOD = 9
GRID_W = 64
ROPE_BASE = 10000.0
DA_HEADS, DA_HD, DA_VD = 4, 64, 128
ML_HEADS, ML_QK, ML_V = 4, 64, 128
GLA_HEADS, GLA_K, GLA_V, GLA_RANK, GLA_TAU = 4, 128, 256, 16, 16.0

LANES = 128
MXU_DIM = 256
VMEM_LIMIT = 56 * 1024 * 1024
NEG = -1e30
LOG2E = math.log2(math.e)

ROW_TILE = 1024
FF_TILE = 256
MIX_TILE = 512
ATT_TQ = 256
ATT_TK = 2048
ATT_VBLK = 256
ML_CHUNK = 256
GLA_CHUNK = 128
GLA_BLOCK = 256


def _mod_index(lay, tm):
    n_lat, per_batch = lay["b"] * lay["seq"] // tm, lay["seq"] // tm
    return lambda i: jnp.where(i < n_lat, 1 + i // per_batch, 0)


def _cparams(sem):
    return pltpu.CompilerParams(dimension_semantics=sem, vmem_limit_bytes=VMEM_LIMIT)


def _sigmoid(x):
    return 1.0 / (1.0 + jnp.exp(-x))


def _silu(x):
    return x * _sigmoid(x)


def _log_sigmoid(x):
    return jnp.minimum(x, 0.0) - jnp.log1p(jnp.exp(-jnp.abs(x)))


def _rms(x, axis=-1):
    return x * lax.rsqrt(jnp.mean(x * x, axis=axis, keepdims=True) + EPS)


def _ada_norm(x, g, shift, scale):
    return _rms(x) * g * (1.0 + scale) + shift


def _dot(a, b):
    return jnp.dot(a, b, preferred_element_type=F32)


def _dot_nt(a, b):
    return lax.dot_general(a, b, (((1,), (1,)), ((), ())), preferred_element_type=F32)


def _dot_tn(a, b):
    return lax.dot_general(a, b, (((0,), (0,)), ((), ())), preferred_element_type=F32)


def _scan_lanes(x, reverse, op, fill):
    n = x.shape[1]
    lane = lax.broadcasted_iota(jnp.int32, x.shape, 1)
    k = 1
    while k < n:
        if reverse:
            x = op(x, jnp.where(lane < n - k, pltpu.roll(x, n - k, 1), fill))
        else:
            x = op(x, jnp.where(lane >= k, pltpu.roll(x, k, 1), fill))
        k *= 2
    return x


def _scan_rows(x, reverse, op, fill):
    n = x.shape[0]
    row = lax.broadcasted_iota(jnp.int32, x.shape, 0)
    k = 1
    while k < n:
        if reverse:
            x = op(x, jnp.where(row < n - k, pltpu.roll(x, n - k, 0), fill))
        else:
            x = op(x, jnp.where(row >= k, pltpu.roll(x, k, 0), fill))
        k *= 2
    return x


def _tri_cumsum(tri, x):
    if x.dtype == BF16:
        return _dot(tri, x)
    hi = x.astype(BF16)
    lo = (x - hi.astype(F32)).astype(BF16)
    r = _dot(tri, jnp.concatenate([hi, lo], axis=1))
    w = x.shape[1]
    return r[:, 0:w] + r[:, w:2 * w]


def _mod_kernel(s_ref, w_ref, b_ref, o_ref):
    s = _silu(s_ref[...])
    o_ref[0] = jnp.dot(s, w_ref[0], preferred_element_type=F32,
                       precision=lax.Precision.HIGHEST) + b_ref[0]


def _modulation(svec, ada_w, ada_b):
    depth, d, n = ada_w.shape
    tn = d
    return pl.pallas_call(
        _mod_kernel,
        grid=(depth, n // tn),
        in_specs=[pl.BlockSpec(svec.shape, lambda l, j: (0, 0)),
                  pl.BlockSpec((1, d, tn), lambda l, j: (l, 0, j)),
                  pl.BlockSpec((1, 1, tn), lambda l, j: (l, 0, j))],
        out_specs=pl.BlockSpec((1, svec.shape[0], tn), lambda l, j: (l, 0, j)),
        out_shape=jax.ShapeDtypeStruct((depth, svec.shape[0], n), F32),
        compiler_params=_cparams(("parallel", "parallel")),
    )(svec, ada_w, ada_b.reshape(depth, 1, n))


def _group_rms(x, width):
    outs = []
    for h in range(x.shape[1] // width):
        outs.append(_rms(x[:, h * width:(h + 1) * width]))
    return jnp.concatenate(outs, axis=1)


def _ffn_kernel(*refs, base, final, n_first, mix):
    x_ref, c_ref, mod_ref, g_ref, wi_ref, wo_ref, fg_ref = refs[:7]
    o_ref, u_ref = refs[-2:]
    dff = wo_ref.shape[0]
    x = x_ref[...]
    if n_first is not None:
        x = jnp.where(pl.program_id(0) < n_first, x, c_ref[...])
    if mix == "even":
        a_ref, mf_ref, mb_ref, op_ref, gm_ref, wm_ref = refs[7:13]
        mem = mf_ref[...].astype(F32) + mb_ref[...].astype(F32)
        m = _group_rms(mem, ML_V) * gm_ref[...] * _sigmoid(op_ref[...].astype(F32))
        y = _dot(a_ref[...], wm_ref[0:512, :]) + _dot(m.astype(BF16), wm_ref[512:1024, :])
        x = x + mod_ref[0, 5:6, :] * y
    elif mix == "odd":
        of_ref, ob_ref, r_ref, gm_ref, wm_ref = refs[7:12]
        o = of_ref[...].astype(F32) + ob_ref[...].astype(F32)
        y = _group_rms(o, GLA_V) * gm_ref[...] * _silu(r_ref[...].astype(F32))
        x = x + mod_ref[0, 5:6, :] * _dot(y.astype(BF16), wm_ref[...])
    h = _ada_norm(x, g_ref[...], mod_ref[0, base:base + 1, :], mod_ref[0, base + 1:base + 2, :]).astype(BF16)
    for j in range(dff // FF_TILE):
        a = _dot(h, wi_ref[:, j * FF_TILE:(j + 1) * FF_TILE])
        b = _dot(h, wi_ref[:, dff + j * FF_TILE:dff + (j + 1) * FF_TILE])
        u_ref[:, j * FF_TILE:(j + 1) * FF_TILE] = (_silu(a) * b).astype(BF16)
    y = x + 0.5 * mod_ref[0, base + 2:base + 3, :] * _dot(u_ref[...], wo_ref[...])
    if final:
        y = _rms(y) * fg_ref[...]
    o_ref[...] = y


def _ffn(xs, mod, g, w_in, w_out, widx, final_g, *, lay, tm, n_rows, base, final, tail=None, mix=None, mix_args=()):
    d = xs.shape[1]
    dff = w_out.shape[2]
    n_tiles = n_rows // tm
    wmap = lambda i: widx + (0, 0)
    midx = _mod_index(lay, tm)
    const = lambda i: (0, 0)
    row = lambda i: (i, 0)
    once = pl.Buffered(1)
    if tail is None:
        n_first, tail, xmap, cspec = None, xs, row, pl.BlockSpec((8, d), const)
    else:
        n_first = xs.shape[0] // tm
        xmap = lambda i: (jnp.minimum(i, n_first - 1), 0)
        cspec = pl.BlockSpec((tm, d), lambda i: (jnp.maximum(i - n_first, 0), 0))
    mix_specs = []
    for arr in mix_args:
        if arr.shape[0] == 1:
            mix_specs.append(pl.BlockSpec(arr.shape, const))
        elif arr.shape[0] == d and arr.shape[1] == d:
            mix_specs.append(pl.BlockSpec(arr.shape, const, pipeline_mode=once))
        else:
            mix_specs.append(pl.BlockSpec((tm, arr.shape[1]), row))
    return pl.pallas_call(
        functools.partial(_ffn_kernel, base=base, final=final, n_first=n_first, mix=mix),
        grid=(n_tiles,),
        in_specs=[pl.BlockSpec((tm, d), xmap),
                  cspec,
                  pl.BlockSpec((1, N_MOD, d), lambda i: (midx(i), 0, 0)),
                  pl.BlockSpec((1, d), const),
                  pl.BlockSpec((None, None) + w_in.shape[2:], wmap, pipeline_mode=once),
                  pl.BlockSpec((None, None) + w_out.shape[2:], wmap, pipeline_mode=once),
                  pl.BlockSpec((1, d), const)] + mix_specs,
        out_specs=pl.BlockSpec((tm, d), row),
        out_shape=jax.ShapeDtypeStruct((n_rows, d), F32),
        scratch_shapes=[pltpu.VMEM((tm, dff), BF16)],
        compiler_params=_cparams(("parallel",)),
    )(xs, tail, mod, g.reshape(1, d), w_in, w_out, final_g.reshape(1, d), *mix_args)


def _rope_heads(r, c, slo, shi, scale):
    outs = []
    for h in range(DA_HEADS):
        rh = r[:, h * LANES:(h + 1) * LANES]
        y = rh * c + pltpu.roll(rh, LANES - 16, 1) * slo + pltpu.roll(rh, 16, 1) * shi
        outs.append(y * scale if scale != 1.0 else y)
    return jnp.concatenate(outs, axis=1)


def _even_proj_kernel(x_ref, xp_ref, xn_ref, mod_ref, g_ref, w_ref, gb_ref, c_ref, slo_ref, shi_ref, cw_ref, cb_ref,
                      q_ref, k_ref, vt_ref, qm_ref, km_ref, mlv_ref, opre_ref, gate_ref, *, lay):
    norm = lambda x: _ada_norm(x, g_ref[...], mod_ref[0, 3:4, :], mod_ref[0, 4:5, :]).astype(BF16)
    h = norm(x_ref[...])
    c, slo, shi = c_ref[...], slo_ref[...], shi_ref[...]
    q_ref[...] = _rope_heads(_dot(h, w_ref[:, 0:512]), c, slo, shi, DA_HD ** -0.5 * LOG2E).astype(BF16)
    k_ref[...] = _rope_heads(_dot(h, w_ref[:, 512:1024]), c, slo, shi, 1.0).astype(BF16)
    vt = _dot(h, w_ref[:, 1024:1536]).T.astype(BF16)
    for i in range(vt_ref.shape[0]):
        vt_ref[i] = vt[:, i * ATT_VBLK:(i + 1) * ATT_VBLK]
    i = pl.program_id(0)
    tm = x_ref.shape[0]
    seq_len = jnp.where(i < lay["b"] * lay["seq"] // tm, lay["seq"], lay["ctx"])
    x = _dot(h, w_ref[:, 1536:2048])
    edge = _dot(norm(jnp.concatenate([xp_ref[...], xn_ref[...]], axis=0)), w_ref[:, 1536:2048])
    rows = lax.broadcasted_iota(jnp.int32, x.shape, 0)
    pos = (i * tm + rows) & (seq_len - 1)
    up = jnp.where(rows == 0, edge[7:8, :], pltpu.roll(x, 1, 0))
    dn = jnp.where(rows == tm - 1, edge[8:9, :], pltpu.roll(x, tm - 1, 0))
    up = jnp.where(pos == 0, 0.0, up)
    dn = jnp.where(pos == seq_len - 1, 0.0, dn)
    y = _silu(cw_ref[0:1, :] * up + cw_ref[1:2, :] * x + cw_ref[2:3, :] * dn + cb_ref[...])
    qm_ref[...] = (y[:, 0:256] * (ML_QK ** -0.5)).astype(BF16)
    km_ref[...] = y[:, 256:512].astype(BF16)
    mlv_ref[...] = _dot(h, w_ref[:, 2048:2560]).astype(BF16)
    opre_ref[...] = _dot(h, w_ref[:, 2560:3072]).astype(BF16)
    gate_ref[...] = _dot(h, w_ref[:, 3072:3200]) + gb_ref[...]


def _even_proj(xs, mod, g, w, gate_b, rope, conv_w, conv_b, *, lay):
    r, d = xs.shape
    tm = lay["tm"]
    n_tiles = r // tm
    sub, last = tm // 8, r // 8 - 1
    midx, ridx = lay["mod_index"], lay["rope_index"]
    row = lambda i: (i, 0)
    const = lambda i: (0, 0)
    tab = pl.BlockSpec((tm, LANES), lambda i: (ridx(i), 0))
    return pl.pallas_call(
        functools.partial(_even_proj_kernel, lay=lay),
        grid=(n_tiles,),
        in_specs=[pl.BlockSpec((tm, d), row),
                  pl.BlockSpec((8, d), lambda i: (jnp.maximum(i * sub - 1, 0), 0)),
                  pl.BlockSpec((8, d), lambda i: (jnp.minimum((i + 1) * sub, last), 0)),
                  pl.BlockSpec((1, N_MOD, d), lambda i: (midx(i), 0, 0)),
                  pl.BlockSpec((1, d), const),
                  pl.BlockSpec(w.shape, const),
                  pl.BlockSpec((1, LANES), const),
                  tab, tab, tab,
                  pl.BlockSpec((3, 512), const), pl.BlockSpec((1, 512), const)],
        out_specs=[pl.BlockSpec((tm, 512), row), pl.BlockSpec((tm, 512), row),
                   pl.BlockSpec((tm // ATT_VBLK, 512, ATT_VBLK), lambda i: (i, 0, 0)),
                   pl.BlockSpec((tm, 256), row), pl.BlockSpec((tm, 256), row), pl.BlockSpec((tm, 512), row),
                   pl.BlockSpec((tm, 512), row), pl.BlockSpec((tm, LANES), row)],
        out_shape=[jax.ShapeDtypeStruct((r, 512), BF16), jax.ShapeDtypeStruct((r, 512), BF16),
                   jax.ShapeDtypeStruct((r // ATT_VBLK, 512, ATT_VBLK), BF16),
                   jax.ShapeDtypeStruct((r, 256), BF16), jax.ShapeDtypeStruct((r, 256), BF16),
                   jax.ShapeDtypeStruct((r, 512), BF16),
                   jax.ShapeDtypeStruct((r, 512), BF16), jax.ShapeDtypeStruct((r, LANES), F32)],
        compiler_params=_cparams(("parallel",)),
    )(xs, xs, xs, mod, g.reshape(1, d), w, gate_b, *rope, conv_w, conv_b.reshape(1, 512))


def _attn_kernel(lam_ref, g_ref, q_ref, kl_ref, kc_ref, vl_ref, vc_ref, o_ref, acc_ref, m_ref, l_ref,
                 *, nql, n_lat_chunks, n_ctx_blocks, lam_init):
    t = pl.program_id(1)
    tq, vb = ATT_TQ, ATT_VBLK
    per = ATT_TK // vb
    m_ref[...] = jnp.full(m_ref.shape, NEG, F32)
    l_ref[...] = jnp.zeros_like(l_ref)
    acc_ref[...] = jnp.zeros_like(acc_ref)

    q = q_ref[...]
    lane = lax.broadcasted_iota(jnp.int32, (tq, LANES), 1)
    qcat = []
    for h in range(DA_HEADS):
        qh = q[:, h * LANES:(h + 1) * LANES]
        qcat.append(jnp.concatenate([jnp.where(lane < DA_HD, qh, jnp.zeros_like(qh)),
                                     jnp.where(lane >= DA_HD, qh, jnp.zeros_like(qh))], axis=0))

    def block(k_of, v_of):
        ss = [_dot_nt(k_of(h), qcat[h]) for h in range(DA_HEADS)]
        alphas, ps = [], []
        for h in range(DA_HEADS):
            m_old = m_ref[h]
            m_new = jnp.maximum(m_old, jnp.max(ss[h], axis=0, keepdims=True))
            alpha = jnp.exp2(m_old - m_new)
            p = jnp.exp2(ss[h] - m_new)
            l_ref[h] = alpha * l_ref[h] + jnp.sum(p, axis=0, keepdims=True)
            m_ref[h] = m_new
            alphas.append(alpha)
            ps.append(p.astype(BF16))
        for h in range(DA_HEADS):
            acc_ref[h] = alphas[h] * acc_ref[h] + _dot(v_of(h), ps[h])

    hs = [slice(h * LANES, (h + 1) * LANES) for h in range(DA_HEADS)]

    @pl.when(t >= nql)
    def _():
        block(lambda h: kc_ref[:, hs[h]],
              lambda h: jnp.concatenate([vc_ref[i, hs[h], :] for i in range(n_ctx_blocks)], axis=1))

    @pl.when(t < nql)
    def _():
        for j in range(n_lat_chunks):
            rows = slice(j * ATT_TK, (j + 1) * ATT_TK)
            vparts = lambda h, j=j: [vl_ref[j * per + i, hs[h], :] for i in range(per)]
            if j == 0:
                block(lambda h: jnp.concatenate([kc_ref[:, hs[h]], kl_ref[rows, hs[h]]], axis=0),
                      lambda h: jnp.concatenate([vc_ref[i, hs[h], :] for i in range(n_ctx_blocks)] + vparts(h), axis=1))
            else:
                block(lambda h, rows=rows: kl_ref[rows, hs[h]], lambda h: jnp.concatenate(vparts(h), axis=1))

    lp = lam_ref[...]
    lam = (jnp.exp(jnp.sum(lp[0:1] * lp[1:2], axis=1, keepdims=True))
           - jnp.exp(jnp.sum(lp[2:3] * lp[3:4], axis=1, keepdims=True)) + lam_init)
    g = g_ref[...] * (1.0 - lam_init)
    for h in range(DA_HEADS):
        on = acc_ref[h] / l_ref[h]
        y = _rms(on[:, 0:tq] - lam * on[:, tq:2 * tq], axis=0) * g
        o_ref[:, h * LANES:(h + 1) * LANES] = y.T.astype(BF16)


def _attention(q, k, vt, diff_lambda, diff_norm_g, *, lay, lam_init):
    bsz, seq, ctx = lay["b"], lay["seq"], lay["ctx"]
    r_lat = bsz * seq
    tq, vb = ATT_TQ, ATT_VBLK
    nql, nqc, nkl, nkc = seq // tq, ctx // tq, seq // vb, ctx // vb
    nlq = r_lat // tq
    nctx0 = r_lat // ctx

    def qidx(b, t):
        return (jnp.where(t < nql, b * nql + t, nlq + b * nqc + (t - nql)), 0)

    return pl.pallas_call(
        functools.partial(_attn_kernel, nql=nql, n_lat_chunks=seq // ATT_TK, n_ctx_blocks=nkc, lam_init=lam_init),
        grid=(bsz, nql + nqc),
        in_specs=[pl.BlockSpec((4, DA_HD), lambda b, t: (0, 0)),
                  pl.BlockSpec((DA_VD, 1), lambda b, t: (0, 0)),
                  pl.BlockSpec((tq, 512), qidx),
                  pl.BlockSpec((seq, 512), lambda b, t: (b, 0)),
                  pl.BlockSpec((ctx, 512), lambda b, t: (nctx0 + b, 0)),
                  pl.BlockSpec((nkl, 512, vb), lambda b, t: (b, 0, 0)),
                  pl.BlockSpec((nkc, 512, vb), lambda b, t: (nctx0 + b, 0, 0))],
        out_specs=pl.BlockSpec((tq, 512), qidx),
        out_shape=jax.ShapeDtypeStruct(q.shape, BF16),
        scratch_shapes=[pltpu.VMEM((DA_HEADS, DA_VD, 2 * tq), F32), pltpu.VMEM((DA_HEADS, 1, 2 * tq), F32),
                        pltpu.VMEM((DA_HEADS, 1, 2 * tq), F32)],
        compiler_params=_cparams(("parallel", "parallel")),
    )(diff_lambda, diff_norm_g.reshape(DA_VD, 1), q, k, k, vt, vt)


def _scan_indices(lay, chunk):
    bsz, seq, ctx = lay["b"], lay["seq"], lay["ctx"]
    nc_lat, nc_ctx = seq // chunk, ctx // chunk
    nl = bsz * nc_lat

    def fwd(b, j):
        return jnp.where(j < nc_ctx, nl + b * nc_ctx + j, b * nc_lat + (j - nc_ctx))

    def bwd(b, j):
        return jnp.where(j < nc_ctx, nl + b * nc_ctx + (nc_ctx - 1 - j), b * nc_lat + (nc_lat - 1 - (j - nc_ctx)))

    return fwd, bwd, nc_ctx + nc_lat


def _mlstm_step(dirs, ctn_ref, m_ref, s_ref, r_ref):
    n = dirs[0][0].shape[0]
    rows = lax.broadcasted_iota(jnp.int32, (n, n), 0)
    cols = lax.broadcasted_iota(jnp.int32, (n, n), 1)
    lane_q = lax.broadcasted_iota(jnp.int32, (n, ML_HEADS * ML_QK), 1)
    heads = [(lane_q >= h * ML_QK) & (lane_q < (h + 1) * ML_QK) for h in range(ML_HEADS)]
    pairs = [(d, h) for d in range(2) for h in range(ML_HEADS)]
    ones = jnp.ones((n, LANES), BF16)

    for d in range(2):
        q = dirs[d][0]
        qstack = jnp.concatenate([jnp.where(heads[h], q, jnp.zeros_like(q)) for h in range(ML_HEADS)], axis=0)
        s_ref[d] = _dot_nt(qstack, dirs[d][1])
        r_ref[d] = _dot(qstack, ctn_ref[d].astype(BF16))

    pre = []
    for d, (q, k, v, gates, _) in enumerate(dirs):
        edge = 0 if d == 1 else n - 1
        bc = _scan_rows(_log_sigmoid(gates), d == 1, jnp.add, 0.0)
        cp = pltpu.roll(gates, 2 * ML_HEADS, 1) - bc
        m_old = m_ref[d][0:1, :]
        mx = jnp.maximum(m_old, _scan_rows(cp, d == 1, jnp.maximum, NEG))
        mx_end = mx[edge:edge + 1, :]
        m_ref[d] = jnp.broadcast_to(bc[edge:edge + 1, :] + mx_end, (8, LANES))
        pre.append((cp.T, m_old, jnp.exp(m_old - mx_end), mx, bc + mx, jnp.exp(cp - mx_end), ctn_ref[d]))

    s_raw = {(d, h): s_ref[d, h * n:(h + 1) * n, :] for d, h in pairs}
    r2 = {(d, h): r_ref[d, h * n:(h + 1) * n, :] for d, h in pairs}

    p = {}
    for d, h in pairs:
        cpt, _, _, mx, _, _, _ = pre[d]
        c = 2 * ML_HEADS + d * ML_HEADS + h
        mask = (cols >= rows) if d == 1 else (cols <= rows)
        mxb = jnp.broadcast_to(mx[:, c:c + 1], (n, LANES))
        dm = jnp.where(mask, cpt[c:c + 1, :] - jnp.concatenate([mxb] * (n // LANES), axis=1), NEG)
        p[d, h] = (s_raw[d, h] * jnp.exp(dm)).astype(BF16)

    nd = {}
    for d, h in pairs:
        v = dirs[d][2]
        nd[d, h] = _dot(p[d, h], jnp.concatenate([v[:, h * ML_V:(h + 1) * ML_V], ones], axis=1))

    for d, h in pairs:
        _, m_old, _, mx, mt, _, _ = pre[d]
        c = 2 * ML_HEADS + d * ML_HEADS + h
        w = jnp.exp(m_old[:, c:c + 1] - jnp.broadcast_to(mx[:, c:c + 1], (n, LANES)))
        floor = jnp.exp(-jnp.broadcast_to(mt[:, c:c + 1], (n, LANES)))
        tot = nd[d, h] + jnp.concatenate([w, w], axis=1) * r2[d, h]
        hid = tot[:, 0:ML_V] / jnp.maximum(jnp.abs(tot[:, ML_V:2 * ML_V]), floor)
        dirs[d][4][:, h * ML_V:(h + 1) * ML_V] = hid.astype(BF16)

    for d, (q, k, v, gates, _) in enumerate(dirs):
        _, _, decay, _, _, wkc, ctn = pre[d]
        wk = jnp.zeros((n, ML_HEADS * ML_QK), F32)
        for h in range(ML_HEADS):
            c = 2 * ML_HEADS + d * ML_HEADS + h
            wk = jnp.where(heads[h], wkc[:, c:c + 1], wk)
        kw = (k.astype(F32) * wk).astype(BF16)
        u = _dot_tn(kw, jnp.concatenate([v, ones], axis=1))
        for h in range(ML_HEADS):
            r0, c = h * ML_QK, 2 * ML_HEADS + d * ML_HEADS + h
            dec = decay[:, c:c + 1]
            ctn_ref[d, r0:r0 + ML_QK, 0:ML_V] = dec * ctn[r0:r0 + ML_QK, 0:ML_V] + u[r0:r0 + ML_QK, h * ML_V:(h + 1) * ML_V]
            ctn_ref[d, r0:r0 + ML_QK, ML_V:2 * ML_V] = (dec * ctn[r0:r0 + ML_QK, ML_V:2 * ML_V]
                                                        + u[r0:r0 + ML_QK, 4 * ML_V:5 * ML_V])


def _mlstm_kernel(qf_ref, kf_ref, vf_ref, gf_ref, qb_ref, kb_ref, vb_ref, gb_ref, of_ref, ob_ref, ctn_ref, m_ref, s_ref, r_ref):
    @pl.when(pl.program_id(1) == 0)
    def _():
        ctn_ref[...] = jnp.zeros_like(ctn_ref)
        m_ref[...] = jnp.zeros_like(m_ref)

    _mlstm_step([(qf_ref[...], kf_ref[...], vf_ref[...], gf_ref[...], of_ref),
                 (qb_ref[...], kb_ref[...], vb_ref[...], gb_ref[...], ob_ref)], ctn_ref, m_ref, s_ref, r_ref)


def _mlstm(q, k, v, gates, *, lay):
    n = ML_CHUNK
    fwd, bwd, steps = _scan_indices(lay, n)
    r = q.shape[0]

    def specs(idx):
        return [pl.BlockSpec((n, 256), lambda b, j: (idx(b, j), 0)),
                pl.BlockSpec((n, 256), lambda b, j: (idx(b, j), 0)),
                pl.BlockSpec((n, 512), lambda b, j: (idx(b, j), 0)),
                pl.BlockSpec((n, LANES), lambda b, j: (idx(b, j), 0))]

    return pl.pallas_call(
        _mlstm_kernel,
        grid=(lay["b"], steps),
        in_specs=specs(fwd) + specs(bwd),
        out_specs=[pl.BlockSpec((n, 512), lambda b, j: (fwd(b, j), 0)),
                   pl.BlockSpec((n, 512), lambda b, j: (bwd(b, j), 0))],
        out_shape=[jax.ShapeDtypeStruct((r, 512), BF16), jax.ShapeDtypeStruct((r, 512), BF16)],
        scratch_shapes=[pltpu.VMEM((2, 256, 256), F32), pltpu.VMEM((2, 8, LANES), F32),
                        pltpu.VMEM((2, ML_HEADS * n, n), F32), pltpu.VMEM((2, ML_HEADS * n, 256), F32)],
        compiler_params=_cparams(("parallel", "arbitrary")),
    )(q, k, v, gates, q, k, v, gates)


def _gla_step(dirs, st_ref):
    n = dirs[0][0].shape[0]
    rows = lax.broadcasted_iota(jnp.int32, (n, n), 0)
    cols = lax.broadcasted_iota(jnp.int32, (n, n), 1)
    mid = n // 2
    pairs = [(d, h) for d in range(2) for h in range(GLA_HEADS)]
    ks = [slice(h * GLA_K, (h + 1) * GLA_K) for h in range(GLA_HEADS)]
    vs = [slice(h * GLA_V, (h + 1) * GLA_V) for h in range(GLA_HEADS)]

    masks = [cols <= rows, cols >= rows]
    bcs = [_tri_cumsum(jnp.where(masks[d], 1.0, 0.0).astype(BF16), dirs[d][3]) for d in range(2)]

    qe, ke, qs, kd, dec, sts = {}, {}, {}, {}, {}, {}
    for d, h in pairs:
        q, k, _, la, _ = dirs[d]
        edge = 0 if d == 1 else n - 1
        bc = bcs[d][:, ks[h]]
        bmid, bl = bc[mid:mid + 1, :], bc[edge:edge + 1, :]
        e_up, e_dn = jnp.exp(bc - bmid), jnp.exp(bmid - bc)
        qf, kf = q[:, ks[h]].astype(F32), k[:, ks[h]].astype(F32)
        qe[d, h] = (qf * e_up).astype(BF16)
        ke[d, h] = (kf * e_dn).astype(BF16)
        qs[d, h] = (qf * (e_up * jnp.exp(bmid))).astype(BF16)
        kd[d, h] = (kf * (e_dn * jnp.exp(bl - bmid))).astype(BF16)
        dec[d, h] = jnp.exp(bl)
        sts[d, h] = st_ref[d, h]

    a, inter = {}, {}
    for d, h in pairs:
        a[d, h] = _dot_nt(qe[d, h], ke[d, h])
    for d, h in pairs:
        inter[d, h] = _dot_nt(qs[d, h], sts[d, h].astype(BF16))
    for d, h in pairs:
        a[d, h] = jnp.where(masks[d], a[d, h], 0.0).astype(BF16)
    for d, h in pairs:
        dirs[d][4][:, vs[h]] = (_dot(a[d, h], dirs[d][2][:, vs[h]]) + inter[d, h]).astype(BF16)
    for d, h in pairs:
        st_ref[d, h] = sts[d, h] * dec[d, h] + _dot_tn(dirs[d][2][:, vs[h]], kd[d, h])


def _gla_kernel(qf_ref, kf_ref, vf_ref, af_ref, qb_ref, kb_ref, vb_ref, ab_ref, of_ref, ob_ref, st_ref):
    @pl.when(pl.program_id(1) == 0)
    def _():
        st_ref[...] = jnp.zeros_like(st_ref)

    n = GLA_CHUNK
    per = qf_ref.shape[0] // n
    for c in range(per):
        f, b = pl.ds(c * n, n), pl.ds((per - 1 - c) * n, n)
        _gla_step([(qf_ref[f, :], kf_ref[f, :], vf_ref[f, :], af_ref[f, :], of_ref.at[f]),
                   (qb_ref[b, :], kb_ref[b, :], vb_ref[b, :], ab_ref[b, :], ob_ref.at[b])], st_ref)


def _gla(q, k, v, la_f, la_b, *, lay):
    n = GLA_BLOCK
    fwd, bwd, steps = _scan_indices(lay, n)
    r = q.shape[0]

    def specs(idx):
        return [pl.BlockSpec((n, 512), lambda b, j: (idx(b, j), 0)),
                pl.BlockSpec((n, 512), lambda b, j: (idx(b, j), 0)),
                pl.BlockSpec((n, 1024), lambda b, j: (idx(b, j), 0)),
                pl.BlockSpec((n, 512), lambda b, j: (idx(b, j), 0))]

    return pl.pallas_call(
        _gla_kernel,
        grid=(lay["b"], steps),
        in_specs=specs(fwd) + specs(bwd),
        out_specs=[pl.BlockSpec((n, 1024), lambda b, j: (fwd(b, j), 0)),
                   pl.BlockSpec((n, 1024), lambda b, j: (bwd(b, j), 0))],
        out_shape=[jax.ShapeDtypeStruct((r, 1024), BF16), jax.ShapeDtypeStruct((r, 1024), BF16)],
        scratch_shapes=[pltpu.VMEM((2, GLA_HEADS, GLA_V, GLA_K), F32)],
        compiler_params=_cparams(("parallel", "arbitrary")),
    )(q, k, v, la_f, q, k, v, la_b)


def _odd_proj_kernel(x_ref, mod_ref, g_ref, w_ref, wg_ref, bg_ref, q_ref, k_ref, v_ref, r_ref, laf_ref, lab_ref):
    h = _ada_norm(x_ref[...], g_ref[...], mod_ref[0, 3:4, :], mod_ref[0, 4:5, :]).astype(BF16)
    q_ref[...] = (_dot(h, w_ref[:, 0:512]) * (GLA_K ** -0.5)).astype(BF16)
    k_ref[...] = _dot(h, w_ref[:, 512:1024]).astype(BF16)
    v_ref[...] = _dot(h, w_ref[:, 1024:2048]).astype(BF16)
    r_ref[...] = _dot(h, w_ref[:, 2048:3072]).astype(BF16)
    lr = _dot(h, w_ref[:, 3072:3200]).astype(BF16)
    la = _log_sigmoid((_dot(lr, wg_ref[...]) + bg_ref[...]).astype(BF16)) * (1.0 / GLA_TAU)
    laf_ref[...] = la[:, 0:512]
    lab_ref[...] = la[:, 512:1024]


def _odd_proj(xs, mod, g, w, wg, bg, *, lay):
    r, d = xs.shape
    tm = lay["tm"]
    midx = lay["mod_index"]
    row = lambda i: (i, 0)
    const = lambda i: (0, 0)
    return pl.pallas_call(
        _odd_proj_kernel,
        grid=(r // tm,),
        in_specs=[pl.BlockSpec((tm, d), row),
                  pl.BlockSpec((1, N_MOD, d), lambda i: (midx(i), 0, 0)),
                  pl.BlockSpec((1, d), const), pl.BlockSpec(w.shape, const),
                  pl.BlockSpec(wg.shape, const), pl.BlockSpec((1, 1024), const)],
        out_specs=[pl.BlockSpec((tm, 512), row), pl.BlockSpec((tm, 512), row), pl.BlockSpec((tm, 1024), row),
                   pl.BlockSpec((tm, 1024), row), pl.BlockSpec((tm, 512), row), pl.BlockSpec((tm, 512), row)],
        out_shape=[jax.ShapeDtypeStruct((r, 512), BF16), jax.ShapeDtypeStruct((r, 512), BF16),
                   jax.ShapeDtypeStruct((r, 1024), BF16), jax.ShapeDtypeStruct((r, 1024), BF16),
                   jax.ShapeDtypeStruct((r, 512), BF16), jax.ShapeDtypeStruct((r, 512), BF16)],
        compiler_params=_cparams(("parallel",)),
    )(xs, mod, g.reshape(1, d), w, wg, bg)


def _rope_tables(seq, tm):
    half = DA_HD // 4
    f32 = np.float32
    row = np.repeat(np.arange(seq // GRID_W), GRID_W).astype(f32)
    col = (np.arange(seq) % GRID_W).astype(f32)
    inv = (f32(ROPE_BASE) ** (-np.arange(half, dtype=f32) * f32(2.0) / f32(2 * half))).astype(f32)
    ar, ac = row[:, None] * inv, col[:, None] * inv
    cos64 = np.concatenate([np.cos(ar), np.cos(ar), np.cos(ac), np.cos(ac)], axis=1)
    sin_r, sin_c, z = np.sin(ar), np.sin(ac), np.zeros_like(ar)
    slo64 = np.concatenate([-sin_r, z, -sin_c, z], axis=1)
    shi64 = np.concatenate([z, sin_r, z, sin_c], axis=1)

    def full(t, fill):
        t = np.concatenate([t, t], axis=1)
        return np.concatenate([t, np.full((tm, LANES), fill, f32)], axis=0).astype(f32)

    return full(cos64, 1.0), full(slo64, 0.0), full(shi64, 0.0)


def _pad_cols(w, n):
    return jnp.pad(w, ((0, 0), (0, n - w.shape[1])))


def kernel(x, c, ctx, c_ctx, ada_w, ada_b, norm_g, ffn_w_in, ffn_w_out, even_w_in, even_w_out, diff_lambda,
           diff_norm_g, mlstm_conv_w, mlstm_conv_b, mlstm_gate_b, mlstm_norm_g, odd_w_in, odd_w_out,
           gla_w_gate, gla_b_gate, gla_norm_g, final_g):
    bsz, seq, d = x.shape
    n_ctx = ctx.shape[1]
    depth = ada_w.shape[0]
    r_lat, r_ctx = bsz * seq, bsz * n_ctx
    tm = min(ROW_TILE, seq, r_ctx)
    assert seq % tm == 0 and r_ctx % tm == 0 and tm % ATT_VBLK == 0 and seq % ATT_TK == 0
    assert seq % ML_CHUNK == 0 and n_ctx % ML_CHUNK == 0 and n_ctx % ATT_TQ == 0 and r_lat % n_ctx == 0
    assert seq & (seq - 1) == 0 and n_ctx & (n_ctx - 1) == 0 and n_ctx % GLA_BLOCK == 0
    n_lat_tiles, per_batch = r_lat // tm, seq // tm
    lay = {
        "b": bsz, "seq": seq, "ctx": n_ctx, "tm": tm,
        "mod_index": lambda i: jnp.where(i < n_lat_tiles, 1 + i // per_batch, 0),
        "rope_index": lambda i: jnp.where(i < n_lat_tiles, i % per_batch, per_batch),
    }
    n_all = (r_lat + r_ctx) // tm

    n_s = -(-(bsz + 1) // 8) * 8
    svec = jnp.zeros((n_s, d), F32).at[:bsz].set(c).at[bsz].set(c_ctx)
    mods = _modulation(svec, ada_w, ada_b)
    mods = jnp.concatenate([mods[:, bsz:bsz + 1], mods[:, :bsz]], axis=1).reshape(depth, bsz + 1, N_MOD, d)

    rope = _rope_tables(seq, tm)
    xs, tail = x.reshape(r_lat, d), ctx.reshape(r_ctx, d)
    w_in_b = ffn_w_in.astype(BF16)
    w_out_b = ffn_w_out.astype(BF16)

    for l in range(depth):
        last = l == depth - 1
        mod = mods[l]
        xs = _ffn(xs, mod, norm_g[l, 0], w_in_b, w_out_b, (l, 0), final_g, lay=lay, tm=tm, n_rows=r_lat + r_ctx,
                  base=0, final=False, tail=tail if l == 0 else None)
        n_keep = r_lat if last else r_lat + r_ctx
        if l % 2 == 0:
            e = l // 2
            lam_init = 0.8 - 0.6 * math.exp(-0.3 * l)
            w = _pad_cols(even_w_in[e], 3200).astype(BF16)
            gate_b = _pad_cols(mlstm_gate_b[e].reshape(1, -1), LANES)
            q, k, vt, qm, km, mlv, opre, gates = _even_proj(xs, mod, norm_g[l, 1], w, gate_b, rope,
                                                           mlstm_conv_w[e], mlstm_conv_b[e], lay=lay)
            a = _attention(q, k, vt, diff_lambda[e], diff_norm_g[e], lay=lay, lam_init=lam_init)
            mf, mb = _mlstm(qm, km, mlv, gates, lay=lay)
            mix, mix_args = "even", (a, mf, mb, opre, mlstm_norm_g[e].reshape(1, 512), even_w_out[e].astype(BF16))
        else:
            o = l // 2
            w = _pad_cols(odd_w_in[o], 3200).astype(BF16)
            wg = jnp.zeros((LANES, 1024), F32)
            wg = wg.at[0:GLA_RANK, 0:512].set(gla_w_gate[o, 0]).at[GLA_RANK:2 * GLA_RANK, 512:1024].set(gla_w_gate[o, 1])
            bg = gla_b_gate[o].reshape(1, 1024)
            q, k, v, rgate, la_f, la_b = _odd_proj(xs, mod, norm_g[l, 1], w, wg.astype(BF16), bg, lay=lay)
            of, ob = _gla(q, k, v, la_f, la_b, lay=lay)
            mix, mix_args = "odd", (of, ob, rgate, jnp.tile(gla_norm_g[o], GLA_HEADS).reshape(1, 1024), odd_w_out[o].astype(BF16))
        xs = _ffn(xs, mod, norm_g[l, 2], w_in_b, w_out_b, (l, 1), final_g, lay=lay, tm=min(MIX_TILE, tm), n_rows=n_keep,
                  base=6, final=last, mix=mix, mix_args=mix_args)
    return xs[:r_lat].reshape(bsz, seq, d)
```
